```python
import math
import jax, jax.numpy as jnp
from jax import lax
import numpy as np

D_MODEL = 1024
BATCH = 1
SEQ = 16384
DEPTH = 4

ATTN_HEAD_DIM = 64
ATTN_HEADS = D_MODEL // (2 * ATTN_HEAD_DIM)
ATTN_WIDTH = ATTN_HEADS * 2 * ATTN_HEAD_DIM
Q_BLOCK = 128
NUM_BUCKETS = 32
MAX_EXACT = NUM_BUCKETS // 2
MAX_DISTANCE = 128
SSD_D_INNER = 2 * D_MODEL
SSD_HEAD_DIM = 64
SSD_HEADS = SSD_D_INNER // SSD_HEAD_DIM
SSD_GROUPS = 4
SSD_D_STATE = 128
SSD_CONV = 4
SSD_CHUNK = 128
SSD_CONV_DIM = SSD_D_INNER + 2 * SSD_GROUPS * SSD_D_STATE
SSD_IN_DIM = 2 * SSD_D_INNER + 2 * SSD_GROUPS * SSD_D_STATE + SSD_HEADS
FFN_DIM = D_MODEL * 7 // 2
N_EXPERTS = 8
TOP_K = 2
NORM_EPS = 1e-6

kernel_name = "hybrid_diffattn_ssd_moe"


def rms_norm(x, w):
    xf = x.astype(jnp.float32)
    y = xf * lax.rsqrt(jnp.mean(xf * xf, axis=-1, keepdims=True) + NORM_EPS)
    return (y * w.astype(jnp.float32)).astype(x.dtype)


def lambda_init_for(layer_idx):
    return 0.8 - 0.6 * math.exp(-0.3 * layer_idx)


def t5_bucket(dist):
    n = jnp.maximum(dist, 0)
    large = MAX_EXACT + (jnp.log(jnp.maximum(n, 1).astype(jnp.float32) / MAX_EXACT)
                         / math.log(MAX_DISTANCE / MAX_EXACT)
                         * (NUM_BUCKETS - MAX_EXACT)).astype(jnp.int32)
    large = jnp.minimum(large, NUM_BUCKETS - 1)
    return jnp.where(n < MAX_EXACT, n, large)


def diff_attention(h, w_in, q_gain, k_gain, lam_params, subln_w, w_out, rel_bias, lambda_init):
    b, s, _ = h.shape
    d = ATTN_HEAD_DIM
    q, k, v = jnp.split(h @ w_in, 3, axis=-1)
    q = rms_norm(q.reshape(b, s, ATTN_HEADS, 2, d), q_gain).transpose(3, 0, 2, 1, 4)
    k = rms_norm(k.reshape(b, s, ATTN_HEADS, 2, d), k_gain).transpose(3, 0, 2, 1, 4)
    q1, q2 = q[0], q[1]
    k1, k2 = k[0], k[1]
    v = v.reshape(b, s, ATTN_HEADS, 2 * d).transpose(0, 2, 1, 3)
    lp = lam_params.astype(jnp.float32)
    lam = jnp.exp(jnp.sum(lp[0] * lp[1])) - jnp.exp(jnp.sum(lp[2] * lp[3])) + lambda_init
    n_blocks = s // Q_BLOCK
    pos = jnp.arange(s, dtype=jnp.int32)
    scale = d ** -0.5

    def to_blocks(t):
        return t.reshape(b, ATTN_HEADS, n_blocks, Q_BLOCK, d).transpose(2, 0, 1, 3, 4)

    def block(args):
        q1b, q2b, qpos = args
        dist = qpos[:, None] - pos[None, :]
        bias = jnp.transpose(rel_bias[t5_bucket(dist)], (2, 0, 1)).astype(jnp.float32)
        visible = dist >= 0

        def probs(qb, kk):
            logits = jnp.einsum('bhqd,bhkd->bhqk', qb, kk).astype(jnp.float32) * scale + bias
            return jax.nn.softmax(jnp.where(visible, logits, -jnp.inf), axis=-1)

        attn = probs(q1b, k1) - lam * probs(q2b, k2)
        return jnp.einsum('bhqk,bhkv->bhqv', attn.astype(v.dtype), v)

    o = lax.map(block, (to_blocks(q1), to_blocks(q2), pos.reshape(n_blocks, Q_BLOCK)))
    o = o.transpose(1, 2, 0, 3, 4).reshape(b, ATTN_HEADS, s, 2 * d)
    o = rms_norm(o, subln_w) * (1.0 - lambda_init)
    o = o.transpose(0, 2, 1, 3).reshape(b, s, ATTN_WIDTH)
    return o @ w_out


def causal_depthwise_conv(x, w, bias):
    y = lax.conv_general_dilated(x, w[:, None, :], window_strides=(1,),
                                 padding=[(w.shape[0] - 1, 0)],
                                 dimension_numbers=('NWC', 'WIO', 'NWC'),
                                 feature_group_count=x.shape[-1])
    return y + bias


def ssd_chunked(x, dt, a, bm, cm):
    b, s, h, p = x.shape
    g, n = bm.shape[-2:]
    r = h // g
    l = SSD_CHUNK
    c = s // l
    xd = (x * dt[..., None]).reshape(b, c, l, g, r, p)
    da = (dt * a).reshape(b, c, l, g, r).transpose(0, 1, 3, 4, 2)
    bm = bm.reshape(b, c, l, g, n)
    cm = cm.reshape(b, c, l, g, n)
    a_cs = jnp.cumsum(da, axis=-1)
    causal = jnp.tril(jnp.ones((l, l), dtype=bool))
    decay_in = jnp.exp(jnp.where(causal, a_cs[..., :, None] - a_cs[..., None, :], -jnp.inf))
    cb = jnp.einsum('bclgn,bcsgn->bcgls', cm, bm)
    y_diag = jnp.einsum('bcgrls,bcsgrp->bclgrp', cb[:, :, :, None] * decay_in, xd)
    decay_states = jnp.exp(a_cs[..., -1:] - a_cs)
    states = jnp.einsum('bclgn,bcgrl,bclgrp->bcgrpn', bm, decay_states, xd)
    chunk_tot = jnp.pad(a_cs[..., -1].transpose(0, 2, 3, 1), ((0, 0), (0, 0), (0, 0), (1, 0)))
    cs2 = jnp.cumsum(chunk_tot, axis=-1)
    causal_c = jnp.tril(jnp.ones((c + 1, c + 1), dtype=bool))
    decay_chunk = jnp.exp(jnp.where(causal_c, cs2[..., :, None] - cs2[..., None, :], -jnp.inf))
    states = jnp.concatenate([jnp.zeros_like(states[:, :1]), states], axis=1)
    states = jnp.einsum('bgrzc,bcgrpn->bzgrpn', decay_chunk, states)[:, :-1]
    y_off = jnp.einsum('bclgn,bcgrpn,bcgrl->bclgrp', cm, states, jnp.exp(a_cs))
    return (y_diag + y_off).reshape(b, s, h, p)


def ssd_mixer(h, w_in, conv_w, conv_b, dt_bias, a_log, d_skip, norm_w, w_out):
    b, s, _ = h.shape
    gn = SSD_GROUPS * SSD_D_STATE
    proj = h @ w_in
    z = proj[..., :SSD_D_INNER]
    xbc = proj[..., SSD_D_INNER:SSD_D_INNER + SSD_CONV_DIM]
    dt = proj[..., SSD_D_INNER + SSD_CONV_DIM:]
    xbc = jax.nn.silu(causal_depthwise_conv(xbc, conv_w, conv_b))
    xs = xbc[..., :SSD_D_INNER].reshape(b, s, SSD_HEADS, SSD_HEAD_DIM).astype(jnp.float32)
    bm = xbc[..., SSD_D_INNER:SSD_D_INNER + gn].reshape(b, s, SSD_GROUPS, SSD_D_STATE).astype(jnp.float32)
    cm = xbc[..., SSD_D_INNER + gn:].reshape(b, s, SSD_GROUPS, SSD_D_STATE).astype(jnp.float32)
    dt = jax.nn.softplus(dt.astype(jnp.float32) + dt_bias.astype(jnp.float32))
    a = -jnp.exp(a_log.astype(jnp.float32))
    y = ssd_chunked(xs, dt, a, bm, cm) + d_skip.astype(jnp.float32)[:, None] * xs
    y = y.reshape(b, s, SSD_D_INNER) * jax.nn.silu(z.astype(jnp.float32))
    y = rms_norm(y.reshape(b, s, SSD_GROUPS, SSD_D_INNER // SSD_GROUPS),
                 norm_w.reshape(SSD_GROUPS, SSD_D_INNER // SSD_GROUPS)).reshape(b, s, SSD_D_INNER)
    return y.astype(h.dtype) @ w_out


def swiglu(h, w_gate, w_up, w_down):
    return (jax.nn.silu(h @ w_gate) * (h @ w_up)) @ w_down


def moe_swiglu(h, router, w_gate, w_up, w_down):
    b, s, d = h.shape
    t = h.reshape(b * s, d)
    logits = (t @ router).astype(jnp.float32)
    top_logits, top_idx = lax.top_k(logits, TOP_K)
    top_w = jax.nn.softmax(top_logits, axis=-1)
    combine = jnp.einsum('tk,tke->te', top_w, jax.nn.one_hot(top_idx, N_EXPERTS, dtype=jnp.float32))
    out = jnp.zeros_like(t)
    for e in range(N_EXPERTS):
        out = out + combine[:, e:e + 1].astype(t.dtype) * swiglu(t, w_gate[e], w_up[e], w_down[e])
    return out.reshape(b, s, d)


def setup_inputs(seed: int = 0) -> dict:
    key = jax.random.key(seed)
    ks = iter(jax.random.split(key, 40))
    n_even = (DEPTH + 1) // 2
    n_odd = DEPTH // 2

    def nrm(shape, scale):
        return jax.random.normal(next(ks), shape, jnp.float32) * scale

    def gain(shape):
        return 1.0 + nrm(shape, 0.02)

    dt0 = jnp.exp(jax.random.uniform(next(ks), (n_odd, SSD_HEADS), jnp.float32)
                  * (math.log(0.1) - math.log(0.001)) + math.log(0.001))
    dt_bias = dt0 + jnp.log(-jnp.expm1(-dt0))
    a_log = jnp.log(jax.random.uniform(next(ks), (n_odd, SSD_HEADS), jnp.float32, minval=1.0, maxval=16.0))
    return {
        "x": nrm((BATCH, SEQ, D_MODEL), 1.0),
        "rel_bias": nrm((NUM_BUCKETS, ATTN_HEADS), 0.5),
        "norm_mix": gain((DEPTH, D_MODEL)),
        "norm_ffn": gain((DEPTH, D_MODEL)),
        "attn_w_in": nrm((n_even, D_MODEL, 3 * ATTN_WIDTH), D_MODEL ** -0.5),
        "attn_q_gain": gain((n_even, ATTN_HEAD_DIM)),
        "attn_k_gain": gain((n_even, ATTN_HEAD_DIM)),
        "attn_lambda": nrm((n_even, 4, ATTN_HEAD_DIM), 0.1),
        "attn_subln": gain((n_even, 2 * ATTN_HEAD_DIM)),
        "attn_w_out": nrm((n_even, ATTN_WIDTH, D_MODEL), ATTN_WIDTH ** -0.5),
        "ffn_w_gate": nrm((n_even, D_MODEL, FFN_DIM), D_MODEL ** -0.5),
        "ffn_w_up": nrm((n_even, D_MODEL, FFN_DIM), D_MODEL ** -0.5),
        "ffn_w_down": nrm((n_even, FFN_DIM, D_MODEL), FFN_DIM ** -0.5),
        "ssd_w_in": nrm((n_odd, D_MODEL, SSD_IN_DIM), D_MODEL ** -0.5),
        "ssd_conv_w": nrm((n_odd, SSD_CONV, SSD_CONV_DIM), SSD_CONV ** -0.5),
        "ssd_conv_b": nrm((n_odd, SSD_CONV_DIM), 0.02),
        "ssd_dt_bias": dt_bias,
        "ssd_a_log": a_log,
        "ssd_d": gain((n_odd, SSD_HEADS)),
        "ssd_norm": gain((n_odd, SSD_D_INNER)),
        "ssd_w_out": nrm((n_odd, SSD_D_INNER, D_MODEL), SSD_D_INNER ** -0.5),
        "moe_router": nrm((n_odd, D_MODEL, N_EXPERTS), D_MODEL ** -0.5),
        "moe_w_gate": nrm((n_odd, N_EXPERTS, D_MODEL, FFN_DIM), D_MODEL ** -0.5),
        "moe_w_up": nrm((n_odd, N_EXPERTS, D_MODEL, FFN_DIM), D_MODEL ** -0.5),
        "moe_w_down": nrm((n_odd, N_EXPERTS, FFN_DIM, D_MODEL), FFN_DIM ** -0.5),
    }


def reference(x, rel_bias, norm_mix, norm_ffn, attn_w_in, attn_q_gain, attn_k_gain, attn_lambda,
              attn_subln, attn_w_out, ffn_w_gate, ffn_w_up, ffn_w_down, ssd_w_in, ssd_conv_w,
              ssd_conv_b, ssd_dt_bias, ssd_a_log, ssd_d, ssd_norm, ssd_w_out, moe_router,
              moe_w_gate, moe_w_up, moe_w_down):
    for i in range(DEPTH):
        j = i // 2
        h = rms_norm(x, norm_mix[i])
        if i % 2 == 0:
            x = x + diff_attention(h, attn_w_in[j], attn_q_gain[j], attn_k_gain[j], attn_lambda[j],
                                   attn_subln[j], attn_w_out[j], rel_bias, lambda_init_for(i))
            h = rms_norm(x, norm_ffn[i])
            x = x + swiglu(h, ffn_w_gate[j], ffn_w_up[j], ffn_w_down[j])
        else:
            x = x + ssd_mixer(h, ssd_w_in[j], ssd_conv_w[j], ssd_conv_b[j], ssd_dt_bias[j],
                              ssd_a_log[j], ssd_d[j], ssd_norm[j], ssd_w_out[j])
            h = rms_norm(x, norm_ffn[i])
            x = x + moe_swiglu(h, moe_router[j], moe_w_gate[j], moe_w_up[j], moe_w_down[j])
    return x
```

```python
import functools
import math

import jax
import jax.numpy as jnp
from jax import lax
from jax.experimental import pallas as pl
from jax.experimental.pallas import tpu as pltpu

F32 = jnp.float32
BF16 = jnp.bfloat16
I32 = jnp.int32

NORM_EPS = 1e-6
LANES = 128
MASK_VALUE = -1e30

ATTN_HEAD_DIM = 64
NUM_BUCKETS = 32
MAX_EXACT = NUM_BUCKETS // 2
MAX_DISTANCE = 128
SSD_HEAD_DIM = 64
SSD_GROUPS = 4
SSD_D_STATE = 128
SSD_CONV = 4
SSD_CHUNK = 128
TOP_K = 2

VMEM_LIMIT = 56 * 1024 * 1024


def _params(*sem):
    return pltpu.CompilerParams(dimension_semantics=sem, vmem_limit_bytes=VMEM_LIMIT)


def _sigmoid(x):
    return 1.0 / (1.0 + jnp.exp(-x))


def _rms_rows(x, gain):
    ms = jnp.mean(x * x, axis=-1, keepdims=True)
    return x * lax.rsqrt(ms + NORM_EPS) * gain


def _norm_matmul_kernel(x_ref, g_ref, w_ref, o_ref, hn_ref):
    @pl.when(pl.program_id(1) == 0)
    def _():
        hn_ref[...] = _rms_rows(x_ref[...], g_ref[...]).astype(BF16)

    o_ref[...] = jnp.dot(hn_ref[...], w_ref[...], preferred_element_type=F32).astype(o_ref.dtype)


def norm_matmul(x, gain, w, out_dtype, tm=512, tn=1024):
    t, d = x.shape
    n = w.shape[1]
    tm = min(tm, t)
    tn = min(tn, n)
    assert t % tm == 0 and n % tn == 0
    return pl.pallas_call(
        _norm_matmul_kernel,
        name="norm_matmul",
        grid=(t // tm, n // tn),
        in_specs=[
            pl.BlockSpec((tm, d), lambda i, j: (i, 0)),
            pl.BlockSpec((1, d), lambda i, j: (0, 0)),
            pl.BlockSpec((d, tn), lambda i, j: (0, j)),
        ],
        out_specs=pl.BlockSpec((tm, tn), lambda i, j: (i, j)),
        out_shape=jax.ShapeDtypeStruct((t, n), out_dtype),
        scratch_shapes=[pltpu.VMEM((tm, d), BF16)],
        compiler_params=_params("parallel", "arbitrary"),
    )(x, gain.reshape(1, d), w)


def _attn_inproj_kernel(x_ref, g_ref, w_ref, qkg_ref, bd_ref, o_ref, hn_ref):
    j = pl.program_id(1)

    @pl.when(j == 0)
    def _():
        hn_ref[...] = _rms_rows(x_ref[...], g_ref[...]).astype(BF16)

    y = jnp.dot(hn_ref[...], w_ref[...], preferred_element_type=F32)

    @pl.when(j < 2)
    def _():
        y2 = (y * y).astype(BF16)
        bd = bd_ref[...]
        w2 = bd.shape[0]
        ms = jnp.concatenate(
            [jnp.dot(y2[:, c * w2:(c + 1) * w2], bd, preferred_element_type=F32)
             for c in range(y.shape[1] // w2)], axis=1)
        o_ref[...] = (y * lax.rsqrt(ms + NORM_EPS) * qkg_ref[0]).astype(o_ref.dtype)

    @pl.when(j == 2)
    def _():
        o_ref[...] = y.astype(o_ref.dtype)


def attn_inproj(x, gain, w_in, q_gain, k_gain, tm=512):
    t, d = x.shape
    width = w_in.shape[1] // 3
    tm = min(tm, t)
    reps = width // ATTN_HEAD_DIM
    qk_gain = jnp.stack([jnp.tile(q_gain, reps) * ATTN_HEAD_DIM ** -0.5,
                         jnp.tile(k_gain, reps)]).reshape(2, 1, width).astype(F32)
    bw = 256
    grp = jnp.arange(bw) // ATTN_HEAD_DIM
    bd = jnp.where(grp[:, None] == grp[None, :], 1.0 / ATTN_HEAD_DIM, 0.0).astype(BF16)
    return pl.pallas_call(
        _attn_inproj_kernel,
        name="attn_inproj",
        grid=(t // tm, 3),
        in_specs=[
            pl.BlockSpec((tm, d), lambda i, j: (i, 0)),
            pl.BlockSpec((1, d), lambda i, j: (0, 0)),
            pl.BlockSpec((d, width), lambda i, j: (0, j)),
            pl.BlockSpec((1, 1, width), lambda i, j: (jnp.minimum(j, 1), 0, 0)),
            pl.BlockSpec((bw, bw), lambda i, j: (0, 0)),
        ],
        out_specs=pl.BlockSpec((tm, width), lambda i, j: (i, j)),
        out_shape=jax.ShapeDtypeStruct((t, 3 * width), BF16),
        scratch_shapes=[pltpu.VMEM((tm, d), BF16)],
        compiler_params=_params("parallel", "arbitrary"),
    )(x, gain.reshape(1, d), w_in, qk_gain, bd)


def _bias_tiles_kernel(tab_ref, o_ref, *, tq):
    h = pl.program_id(0)
    r = lax.broadcasted_iota(I32, (tq, tq), 0)
    c = lax.broadcasted_iota(I32, (tq, tq), 1)
    far = tab_ref[NUM_BUCKETS - 1, h]
    for d in range(2):
        dist = r - c + d * tq
        n = jnp.maximum(dist, 0)
        nf = jnp.maximum(n, 1).astype(F32)
        large = MAX_EXACT + (jnp.log(nf / MAX_EXACT) / math.log(MAX_DISTANCE / MAX_EXACT)
                             * (NUM_BUCKETS - MAX_EXACT)).astype(I32)
        large = jnp.minimum(large, NUM_BUCKETS - 1)
        bucket = jnp.where(n < MAX_EXACT, n, large)
        bias = jnp.zeros((tq, tq), F32)
        for b in range(NUM_BUCKETS):
            bias = jnp.where(bucket == b, tab_ref[b, h], bias)
        o_ref[0, d] = jnp.where(dist >= 0, bias - far, MASK_VALUE)


def bias_tiles(rel_bias, tq):
    nb, h = rel_bias.shape
    return pl.pallas_call(
        functools.partial(_bias_tiles_kernel, tq=tq),
        name="bias_tiles",
        grid=(h,),
        in_specs=[pl.BlockSpec(memory_space=pltpu.SMEM)],
        out_specs=pl.BlockSpec((1, 2, tq, tq), lambda i: (i, 0, 0, 0)),
        out_shape=jax.ShapeDtypeStruct((h, 2, tq, tq), F32),
        compiler_params=_params("parallel"),
    )(rel_bias.astype(F32))


def _attn_kernel(q_ref, k_ref, v_ref, bias_ref, lam_ref, sub_ref, o_ref,
                 m_ref, l_ref, acc_ref, *, tq, lambda_init):
    qi = pl.program_id(1)
    d = ATTN_HEAD_DIM
    q = q_ref[...]
    lane = lax.broadcasted_iota(I32, q.shape, 1)
    zero = jnp.zeros_like(q)
    q2 = jnp.concatenate([jnp.where(lane < d, q, zero), jnp.where(lane >= d, q, zero)], axis=0)

    m_ref[...] = jnp.full(m_ref.shape, MASK_VALUE, F32)
    l_ref[...] = jnp.zeros(l_ref.shape, F32)
    acc_ref[...] = jnp.zeros(acc_ref.shape, F32)

    def step(kstart, bias):
        kb = k_ref[pl.ds(kstart, tq), :]
        vb = v_ref[pl.ds(kstart, tq), :]
        s = lax.dot_general(q2, kb, (((1,), (1,)), ((), ())), preferred_element_type=F32)
        if bias is not None:
            s = s + jnp.concatenate([bias, bias], axis=0)
        m_prev = m_ref[...]
        m_new = jnp.maximum(m_prev, jnp.max(s, axis=-1, keepdims=True))
        p = jnp.exp(s - m_new)
        alpha = jnp.exp(m_prev - m_new)
        l_ref[...] = alpha * l_ref[...] + jnp.sum(p, axis=-1, keepdims=True)
        acc_ref[...] = alpha * acc_ref[...] + jnp.dot(p.astype(BF16), vb, preferred_element_type=F32)
        m_ref[...] = m_new

    def far_body(j, carry):
        step(pl.multiple_of(j * tq, tq), None)
        return carry

    lax.fori_loop(0, jnp.maximum(qi - 1, 0), far_body, 0)

    @pl.when(qi >= 1)
    def _():
        step(pl.multiple_of((qi - 1) * tq, tq), bias_ref[0, 1])

    step(pl.multiple_of(qi * tq, tq), bias_ref[0, 0])

    o = acc_ref[...] / l_ref[...]
    lp = lam_ref[...]
    lam = (jnp.exp(jnp.sum(lp[0:1] * lp[1:2], axis=-1, keepdims=True))
           - jnp.exp(jnp.sum(lp[2:3] * lp[3:4], axis=-1, keepdims=True)) + lambda_init)
    o = o[:tq] - lam * o[tq:]
    o_ref[...] = (_rms_rows(o, sub_ref[...]) * (1.0 - lambda_init)).astype(o_ref.dtype)


def diff_attention_core(qkv, bias, lam_params, subln_w, lambda_init, tq):
    t = qkv.shape[0]
    width = qkv.shape[1] // 3
    hd = 2 * ATTN_HEAD_DIM
    heads = width // hd
    assert t % tq == 0 and tq >= MAX_DISTANCE
    return pl.pallas_call(
        functools.partial(_attn_kernel, tq=tq, lambda_init=lambda_init),
        name="diff_attn",
        grid=(heads, t // tq),
        in_specs=[
            pl.BlockSpec((tq, hd), lambda h, i: (i, h)),
            pl.BlockSpec((t, hd), lambda h, i: (0, heads + h)),
            pl.BlockSpec((t, hd), lambda h, i: (0, 2 * heads + h)),
            pl.BlockSpec((1, 2, tq, tq), lambda h, i: (h, 0, 0, 0)),
            pl.BlockSpec((4, ATTN_HEAD_DIM), lambda h, i: (0, 0)),
            pl.BlockSpec((1, hd), lambda h, i: (0, 0)),
        ],
        out_specs=pl.BlockSpec((tq, hd), lambda h, i: (i, h)),
        out_shape=jax.ShapeDtypeStruct((t, width), BF16),
        scratch_shapes=[pltpu.VMEM((2 * tq, 1), F32), pltpu.VMEM((2 * tq, 1), F32),
                        pltpu.VMEM((2 * tq, hd), F32)],
        compiler_params=_params("parallel", "arbitrary"),
    )(qkv, qkv, qkv, bias, lam_params.astype(F32), subln_w.reshape(1, hd).astype(F32))


def _proj_residual_kernel(x_ref, a_ref, w_ref, o_ref):
    o_ref[...] = x_ref[...] + jnp.dot(a_ref[...], w_ref[...], preferred_element_type=F32)


def proj_residual(x, a, w, tm=512):
    t, d = x.shape
    k = a.shape[1]
    tm = min(tm, t)
    return pl.pallas_call(
        _proj_residual_kernel,
        name="proj_residual",
        grid=(t // tm,),
        in_specs=[
            pl.BlockSpec((tm, d), lambda i: (i, 0)),
            pl.BlockSpec((tm, k), lambda i: (i, 0)),
            pl.BlockSpec((k, d), lambda i: (0, 0)),
        ],
        out_specs=pl.BlockSpec((tm, d), lambda i: (i, 0)),
        out_shape=jax.ShapeDtypeStruct((t, d), F32),
        compiler_params=_params("parallel"),
    )(x, a, w)


def _ffn_kernel(*refs, use_combine):
    if use_combine:
        x_ref, g_ref, cw_ref, wg_ref, wu_ref, wd_ref, o_ref, hn_ref, acc_ref = refs
    else:
        x_ref, g_ref, wg_ref, wu_ref, wd_ref, o_ref, hn_ref, acc_ref = refs
    e = pl.program_id(1)
    j = pl.program_id(2)

    @pl.when((e == 0) & (j == 0))
    def _():
        x = x_ref[...]
        hn_ref[...] = _rms_rows(x, g_ref[...]).astype(BF16)
        acc_ref[...] = x

    hn = hn_ref[...]
    gate = jnp.dot(hn, wg_ref[0], preferred_element_type=F32)
    up = jnp.dot(hn, wu_ref[0], preferred_element_type=F32)
    act = gate * _sigmoid(gate) * up
    if use_combine:
        cw = cw_ref[...]
        lane = lax.broadcasted_iota(I32, cw.shape, 1)
        act = act * jnp.sum(jnp.where(lane == e, cw, 0.0), axis=-1, keepdims=True)
    acc_ref[...] += jnp.dot(act.astype(BF16), wd_ref[0], preferred_element_type=F32)

    @pl.when((e == pl.num_programs(1) - 1) & (j == pl.num_programs(2) - 1))
    def _():
        o_ref[...] = acc_ref[...]


def ffn_residual(x, gain, w_gate, w_up, w_down, combine=None, tm=1024, tf=512):
    t, d = x.shape
    ne, _, f = w_gate.shape
    tm = min(tm, t)
    assert t % tm == 0 and f % tf == 0
    use_combine = combine is not None
    in_specs = [pl.BlockSpec((tm, d), lambda i, e, j: (i, 0)),
                pl.BlockSpec((1, d), lambda i, e, j: (0, 0))]
    args = [x, gain.reshape(1, d)]
    if use_combine:
        in_specs.append(pl.BlockSpec((tm, LANES), lambda i, e, j: (i, 0)))
        args.append(combine)
    in_specs += [pl.BlockSpec((1, d, tf), lambda i, e, j: (e, 0, j)),
                 pl.BlockSpec((1, d, tf), lambda i, e, j: (e, 0, j)),
                 pl.BlockSpec((1, tf, d), lambda i, e, j: (e, j, 0))]
    args += [w_gate, w_up, w_down]
    return pl.pallas_call(
        functools.partial(_ffn_kernel, use_combine=use_combine),
        name="ffn",
        grid=(t // tm, ne, f // tf),
        in_specs=in_specs,
        out_specs=pl.BlockSpec((tm, d), lambda i, e, j: (i, 0)),
        out_shape=jax.ShapeDtypeStruct((t, d), F32),
        scratch_shapes=[pltpu.VMEM((tm, d), BF16), pltpu.VMEM((tm, d), F32)],
        compiler_params=_params("parallel", "arbitrary", "arbitrary"),
    )(*args)


def _router_kernel(x_ref, g_ref, r_ref, cw_ref, *, n_experts):
    hn = _rms_rows(x_ref[...], g_ref[...])
    logits = jnp.dot(hn, r_ref[...], preferred_element_type=F32, precision=lax.Precision.HIGHEST)
    lane = lax.broadcasted_iota(I32, logits.shape, 1)
    lg = jnp.where(lane < n_experts, logits, MASK_VALUE)
    m1 = jnp.max(lg, axis=-1, keepdims=True)
    i1 = jnp.min(jnp.where(lg == m1, lane, LANES), axis=-1, keepdims=True)
    lg2 = jnp.where(lane == i1, MASK_VALUE, lg)
    m2 = jnp.max(lg2, axis=-1, keepdims=True)
    i2 = jnp.min(jnp.where(lg2 == m2, lane, LANES), axis=-1, keepdims=True)
    e2 = jnp.exp(m2 - m1)
    w1 = 1.0 / (1.0 + e2)
    w2 = e2 / (1.0 + e2)
    cw_ref[...] = jnp.where(lane == i1, w1, 0.0) + jnp.where(lane == i2, w2, 0.0)


def router(x, gain, w_router, tm=512):
    t, d = x.shape
    ne = w_router.shape[1]
    tm = min(tm, t)
    r_pad = jnp.zeros((d, LANES), F32).at[:, :ne].set(w_router.astype(F32))
    return pl.pallas_call(
        functools.partial(_router_kernel, n_experts=ne),
        name="router",
        grid=(t // tm,),
        in_specs=[pl.BlockSpec((tm, d), lambda i: (i, 0)),
                  pl.BlockSpec((1, d), lambda i: (0, 0)),
                  pl.BlockSpec((d, LANES), lambda i: (0, 0))],
        out_specs=pl.BlockSpec((tm, LANES), lambda i: (i, 0)),
        out_shape=jax.ShapeDtypeStruct((t, LANES), F32),
        compiler_params=_params("parallel"),
    )(x, gain.reshape(1, d), r_pad)


def _ssd_kernel(xbc_ref, z_ref, dt_ref, cw_ref, cb_ref, dtb_ref, alog_ref, dsk_ref, nw_ref,
                o_ref, prev_ref, state_ref, y_ref, *, d_inner):
    L = SSD_CHUNK
    gn = SSD_GROUPS * SSD_D_STATE
    n_pairs = d_inner // LANES
    pairs_per_group = n_pairs // SSD_GROUPS

    @pl.when(pl.program_id(0) == 0)
    def _():
        prev_ref[...] = jnp.zeros(prev_ref.shape, F32)
        state_ref[...] = jnp.zeros(state_ref.shape, F32)

    x = xbc_ref[...]
    xp = prev_ref[...]
    row1 = lax.broadcasted_iota(I32, (L, 1), 0)
    cw = cw_ref[...]
    acc = cb_ref[...] + cw[SSD_CONV - 1:SSD_CONV] * x
    for k in range(1, SSD_CONV):
        shifted = jnp.where(row1 < k, pltpu.roll(xp, k, 0), pltpu.roll(x, k, 0))
        acc = acc + cw[SSD_CONV - 1 - k:SSD_CONV - k] * shifted
    prev_ref[...] = x
    xc = acc * _sigmoid(acc)

    dtr = dt_ref[...] + dtb_ref[...]
    dtv = jnp.maximum(dtr, 0.0) + jnp.log1p(jnp.exp(-jnp.abs(dtr)))
    da = dtv * (-jnp.exp(alog_ref[...]))
    rowh = lax.broadcasted_iota(I32, da.shape, 0)
    acs = da
    s = 1
    while s < L:
        acs = acs + jnp.where(rowh >= s, pltpu.roll(acs, s, 0), 0.0)
        s *= 2
    acs_t = acs.T
    tot = acs[L - 1:L, :]
    decay_to_end = jnp.exp(tot - acs)
    decay_from_start = jnp.exp(acs)
    tot_col = jnp.exp(acs_t[:, L - 1:L])

    lane = lax.broadcasted_iota(I32, (L, LANES), 1)
    rowl = lax.broadcasted_iota(I32, (L, LANES), 0)
    left = lane < SSD_HEAD_DIM
    causal = rowl >= lane

    def per_head(a, h0):
        return jnp.where(left, a[:, h0:h0 + 1], a[:, h0 + 1:h0 + 2])

    for g in range(SSD_GROUPS):
        bg = xc[:, d_inner + g * SSD_D_STATE:d_inner + (g + 1) * SSD_D_STATE].astype(BF16)
        cg = xc[:, d_inner + gn + g * SSD_D_STATE:d_inner + gn + (g + 1) * SSD_D_STATE].astype(BF16)
        cbm = lax.dot_general(cg, bg, (((1,), (1,)), ((), ())), preferred_element_type=F32)
        for jp in range(pairs_per_group):
            pj = g * pairs_per_group + jp
            h0 = 2 * pj
            xs_pair = xc[:, pj * LANES:(pj + 1) * LANES]
            xd = xs_pair * per_head(dtv, h0)
            y = jnp.zeros((L, LANES), F32)
            for half, hh in enumerate((h0, h0 + 1)):
                diff = acs[:, hh:hh + 1] - acs_t[hh:hh + 1, :]
                dec = jnp.exp(jnp.where(causal, diff, MASK_VALUE))
                in_half = left if half == 0 else jnp.logical_not(left)
                y = y + jnp.dot((cbm * dec).astype(BF16),
                                jnp.where(in_half, xd, 0.0).astype(BF16),
                                preferred_element_type=F32)
            s_old = state_ref[pj]
            y_off = lax.dot_general(cg, s_old.astype(BF16), (((1,), (1,)), ((), ())),
                                    preferred_element_type=F32)
            y = y + y_off * per_head(decay_from_start, h0)
            s_new = lax.dot_general((xd * per_head(decay_to_end, h0)).astype(BF16), bg,
                                    (((0,), (0,)), ((), ())), preferred_element_type=F32)
            carry = jnp.where(rowl < SSD_HEAD_DIM, tot_col[h0:h0 + 1, :], tot_col[h0 + 1:h0 + 2, :])
            state_ref[pj] = carry * s_old + s_new
            y_ref[:, pj * LANES:(pj + 1) * LANES] = y + dsk_ref[:, pj * LANES:(pj + 1) * LANES] * xs_pair

    z = z_ref[...].astype(F32)
    yg = y_ref[...] * (z * _sigmoid(z))
    gw = d_inner // SSD_GROUPS
    for g in range(SSD_GROUPS):
        blk = yg[:, g * gw:(g + 1) * gw]
        o_ref[:, g * gw:(g + 1) * gw] = _rms_rows(blk, nw_ref[:, g * gw:(g + 1) * gw]).astype(o_ref.dtype)


def ssd_core(xbc, z, dt, conv_w, conv_b, dt_bias, a_log, d_skip, norm_w):
    t = xbc.shape[0]
    d_inner = z.shape[1]
    conv_dim = xbc.shape[1]
    heads = d_inner // SSD_HEAD_DIM
    L = SSD_CHUNK
    assert t % L == 0 and heads <= LANES

    def lanes(v):
        return jnp.zeros((1, LANES), F32).at[0, :heads].set(v.astype(F32))

    row = lambda i: (i, 0)
    fixed = lambda i: (0, 0)
    return pl.pallas_call(
        functools.partial(_ssd_kernel, d_inner=d_inner),
        name="ssd",
        grid=(t // L,),
        in_specs=[
            pl.BlockSpec((L, conv_dim), row),
            pl.BlockSpec((L, d_inner), row),
            pl.BlockSpec((L, LANES), row),
            pl.BlockSpec((SSD_CONV, conv_dim), fixed),
            pl.BlockSpec((1, conv_dim), fixed),
            pl.BlockSpec((1, LANES), fixed),
            pl.BlockSpec((1, LANES), fixed),
            pl.BlockSpec((1, d_inner), fixed),
            pl.BlockSpec((1, d_inner), fixed),
        ],
        out_specs=pl.BlockSpec((L, d_inner), row),
        out_shape=jax.ShapeDtypeStruct((t, d_inner), BF16),
        scratch_shapes=[pltpu.VMEM((L, conv_dim), F32),
                        pltpu.VMEM((d_inner // LANES, LANES, SSD_D_STATE), F32),
                        pltpu.VMEM((L, d_inner), F32)],
        compiler_params=_params("arbitrary"),
    )(xbc, z, dt, conv_w.astype(F32), conv_b.reshape(1, conv_dim).astype(F32),
      lanes(dt_bias), lanes(a_log),
      jnp.repeat(d_skip.astype(F32), SSD_HEAD_DIM).reshape(1, d_inner),
      norm_w.reshape(1, d_inner).astype(F32))


def lambda_init_for(layer_idx):
    return 0.8 - 0.6 * math.exp(-0.3 * layer_idx)


def ssd_layer(xt, gain, w_in, conv_w, conv_b, dt_bias, a_log, d_skip, norm_w, w_out):
    d = xt.shape[1]
    d_inner = w_out.shape[0]
    conv_dim = conv_w.shape[1]
    heads = dt_bias.shape[0]
    w_z = w_in[:, :d_inner].astype(BF16)
    w_xbc = w_in[:, d_inner:d_inner + conv_dim].astype(BF16)
    w_dt = jnp.zeros((d, LANES), BF16).at[:, :heads].set(w_in[:, d_inner + conv_dim:].astype(BF16))
    z = norm_matmul(xt, gain, w_z, BF16)
    xbc = norm_matmul(xt, gain, w_xbc, F32)
    dt = norm_matmul(xt, gain, w_dt, F32)
    y = ssd_core(xbc, z, dt, conv_w, conv_b, dt_bias, a_log, d_skip, norm_w)
    return proj_residual(xt, y, w_out.astype(BF16))


def moe_layer(xt, gain, w_router, w_gate, w_up, w_down):
    combine = router(xt, gain, w_router)
    return ffn_residual(xt, gain, w_gate, w_up, w_down, combine=combine)


def kernel(x, rel_bias, norm_mix, norm_ffn, attn_w_in, attn_q_gain, attn_k_gain, attn_lambda, attn_subln, attn_w_out, ffn_w_gate, ffn_w_up, ffn_w_down, ssd_w_in, ssd_conv_w, ssd_conv_b, ssd_dt_bias, ssd_a_log, ssd_d, ssd_norm, ssd_w_out, moe_router, moe_w_gate, moe_w_up, moe_w_down):
    b, s, d = x.shape
    t = b * s
    assert b == 1
    depth = norm_mix.shape[0]
    xt = x.reshape(t, d).astype(F32)
    tq = min(512, t)
    bias = bias_tiles(rel_bias, tq)
    for i in range(depth):
        j = i // 2
        if i % 2 == 0:
            qkv = attn_inproj(xt, norm_mix[i], attn_w_in[j].astype(BF16), attn_q_gain[j], attn_k_gain[j])
            o = diff_attention_core(qkv, bias, attn_lambda[j], attn_subln[j], lambda_init_for(i), tq)
            xt = proj_residual(xt, o, attn_w_out[j].astype(BF16))
            xt = ffn_residual(xt, norm_ffn[i], ffn_w_gate[j][None].astype(BF16),
                              ffn_w_up[j][None].astype(BF16), ffn_w_down[j][None].astype(BF16))
        else:
            xt = ssd_layer(xt, norm_mix[i], ssd_w_in[j], ssd_conv_w[j], ssd_conv_b[j], ssd_dt_bias[j],
                           ssd_a_log[j], ssd_d[j], ssd_norm[j], ssd_w_out[j])
            xt = moe_layer(xt, norm_ffn[i], moe_router[j], moe_w_gate[j].astype(BF16),
                           moe_w_up[j].astype(BF16), moe_w_down[j].astype(BF16))
    return xt.reshape(b, s, d)
```

```python
import functools
import math

import jax
import jax.numpy as jnp
from jax import lax
from jax.experimental import pallas as pl
from jax.experimental.pallas import tpu as pltpu

F32 = jnp.float32
BF16 = jnp.bfloat16
I32 = jnp.int32

NORM_EPS = 1e-6
LANES = 128
MASK_VALUE = -1e30
LOG2E = math.log2(math.e)
ONES_ROWS = 16

ATTN_HEAD_DIM = 64
NUM_BUCKETS = 32
MAX_EXACT = NUM_BUCKETS // 2
MAX_DISTANCE = 128
SSD_HEAD_DIM = 64
SSD_GROUPS = 4
SSD_D_STATE = 128
SSD_CONV = 4
SSD_CHUNK = 128
TOP_K = 2

VMEM_LIMIT = 56 * 1024 * 1024


def _params(*sem):
    return pltpu.CompilerParams(dimension_semantics=sem, vmem_limit_bytes=VMEM_LIMIT)


def _sigmoid(x):
    return 1.0 / (1.0 + jnp.exp(-x))


def _rms_rows(x, gain):
    ms = jnp.mean(x * x, axis=-1, keepdims=True)
    return x * lax.rsqrt(ms + NORM_EPS) * gain


def _norm_matmul_kernel(x_ref, g_ref, w_ref, o_ref, hn_ref):
    @pl.when(pl.program_id(1) == 0)
    def _():
        hn_ref[...] = _rms_rows(x_ref[...], g_ref[...]).astype(BF16)

    o_ref[...] = jnp.dot(hn_ref[...], w_ref[...], preferred_element_type=F32).astype(o_ref.dtype)


def norm_matmul(x, gain, w, out_dtype, tm=512, tn=1024):
    t, d = x.shape
    n = w.shape[1]
    tm = min(tm, t)
    tn = min(tn, n)
    assert t % tm == 0 and n % tn == 0
    return pl.pallas_call(
        _norm_matmul_kernel,
        name="norm_matmul",
        grid=(t // tm, n // tn),
        in_specs=[
            pl.BlockSpec((tm, d), lambda i, j: (i, 0)),
            pl.BlockSpec((1, d), lambda i, j: (0, 0)),
            pl.BlockSpec((d, tn), lambda i, j: (0, j)),
        ],
        out_specs=pl.BlockSpec((tm, tn), lambda i, j: (i, j)),
        out_shape=jax.ShapeDtypeStruct((t, n), out_dtype),
        scratch_shapes=[pltpu.VMEM((tm, d), BF16)],
        compiler_params=_params("parallel", "arbitrary"),
    )(x, gain.reshape(1, d), w)


def _attn_inproj_kernel(x_ref, g_ref, w_ref, qkg_ref, bd_ref, qt_ref, k_ref, vt_ref, hn_ref):
    hn_ref[...] = _rms_rows(x_ref[...], g_ref[...]).astype(BF16)
    hn = hn_ref[...]
    width = k_ref.shape[1]
    bd = bd_ref[...]
    w2 = bd.shape[0]

    def head_norm(y, gain_row):
        y2 = (y * y).astype(BF16)
        ms = jnp.concatenate(
            [jnp.dot(y2[:, c * w2:(c + 1) * w2], bd, preferred_element_type=F32)
             for c in range(width // w2)], axis=1)
        return y * lax.rsqrt(ms + NORM_EPS) * gain_row

    q = jnp.dot(hn, w_ref[:, 0:width], preferred_element_type=F32)
    qt_ref[...] = head_norm(q, qkg_ref[0:1, :]).T.astype(BF16)
    k = jnp.dot(hn, w_ref[:, width:2 * width], preferred_element_type=F32)
    k_ref[...] = head_norm(k, qkg_ref[1:2, :]).astype(BF16)
    v = jnp.dot(hn, w_ref[:, 2 * width:3 * width], preferred_element_type=F32)
    vt = v.astype(BF16).T
    hd = 2 * ATTN_HEAD_DIM
    ones = jnp.ones((ONES_ROWS, vt.shape[1]), BF16)
    for h in range(width // hd):
        r0 = h * (hd + ONES_ROWS)
        vt_ref[r0:r0 + hd, :] = vt[h * hd:(h + 1) * hd, :]
        vt_ref[r0 + hd:r0 + hd + ONES_ROWS, :] = ones


def attn_inproj(x, gain, w_in, q_gain, k_gain, tm=512):
    t, d = x.shape
    width = w_in.shape[1] // 3
    tm = min(tm, t)
    reps = width // ATTN_HEAD_DIM
    qk_gain = jnp.stack([jnp.tile(q_gain, reps) * (ATTN_HEAD_DIM ** -0.5 * LOG2E),
                         jnp.tile(k_gain, reps)]).astype(F32)
    hd = 2 * ATTN_HEAD_DIM
    vt_rows = (width // hd) * (hd + ONES_ROWS)
    bw = 256
    grp = jnp.arange(bw) // ATTN_HEAD_DIM
    bd = jnp.where(grp[:, None] == grp[None, :], 1.0 / ATTN_HEAD_DIM, 0.0).astype(BF16)
    return pl.pallas_call(
        _attn_inproj_kernel,
        name="attn_inproj",
        grid=(t // tm,),
        in_specs=[
            pl.BlockSpec((tm, d), lambda i: (i, 0)),
            pl.BlockSpec((1, d), lambda i: (0, 0)),
            pl.BlockSpec((d, 3 * width), lambda i: (0, 0)),
            pl.BlockSpec((2, width), lambda i: (0, 0)),
            pl.BlockSpec((bw, bw), lambda i: (0, 0)),
        ],
        out_specs=[pl.BlockSpec((width, tm), lambda i: (0, i)),
                   pl.BlockSpec((tm, width), lambda i: (i, 0)),
                   pl.BlockSpec((vt_rows, tm), lambda i: (0, i))],
        out_shape=[jax.ShapeDtypeStruct((width, t), BF16),
                   jax.ShapeDtypeStruct((t, width), BF16),
                   jax.ShapeDtypeStruct((vt_rows, t), BF16)],
        scratch_shapes=[pltpu.VMEM((tm, d), BF16)],
        compiler_params=_params("parallel"),
    )(x, gain.reshape(1, d), w_in, qk_gain, bd)


def _bias_tiles_kernel(tab_ref, o_ref, *, tq):
    h = pl.program_id(0)
    r = lax.broadcasted_iota(I32, (tq, tq), 0)
    c = lax.broadcasted_iota(I32, (tq, tq), 1)
    far = tab_ref[NUM_BUCKETS - 1, h]
    for d in range(2):
        dist = c - r + d * tq
        n = jnp.maximum(dist, 0)
        nf = jnp.maximum(n, 1).astype(F32)
        large = MAX_EXACT + (jnp.log(nf / MAX_EXACT) / math.log(MAX_DISTANCE / MAX_EXACT)
                             * (NUM_BUCKETS - MAX_EXACT)).astype(I32)
        large = jnp.minimum(large, NUM_BUCKETS - 1)
        bucket = jnp.where(n < MAX_EXACT, n, large)
        bias = jnp.zeros((tq, tq), F32)
        for b in range(NUM_BUCKETS):
            bias = jnp.where(bucket == b, tab_ref[b, h], bias)
        o_ref[0, d] = jnp.where(dist >= 0, (bias - far) * LOG2E, MASK_VALUE)


def bias_tiles(rel_bias, tq):
    nb, h = rel_bias.shape
    return pl.pallas_call(
        functools.partial(_bias_tiles_kernel, tq=tq),
        name="bias_tiles",
        grid=(h,),
        in_specs=[pl.BlockSpec(memory_space=pltpu.SMEM)],
        out_specs=pl.BlockSpec((1, 2, tq, tq), lambda i: (i, 0, 0, 0)),
        out_shape=jax.ShapeDtypeStruct((h, 2, tq, tq), F32),
        compiler_params=_params("parallel"),
    )(rel_bias.astype(F32))


def _attn_kernel(qt_ref, k_ref, vt_ref, bias_ref, lam_ref, sub_ref, o_ref,
                 m_ref, acc_ref, qbd_ref, s_ref, *, tq, cw, lambda_init):
    qi = pl.program_id(1)
    d = ATTN_HEAD_DIM
    hd = 2 * d
    qt = qt_ref[...]
    row = lax.broadcasted_iota(I32, qt.shape, 0)
    zero = jnp.zeros_like(qt)
    qbd_ref[:, :tq] = jnp.where(row < d, qt, zero)
    qbd_ref[:, tq:] = jnp.where(row >= d, qt, zero)

    m_ref[...] = jnp.full(m_ref.shape, MASK_VALUE, F32)
    acc_ref[...] = jnp.zeros(acc_ref.shape, F32)

    n_chunks = 2 * tq // cw

    def block_start(j):
        return pl.multiple_of(j * tq, tq)

    def scores(j, c):
        kb = k_ref[pl.ds(block_start(j), tq), :]
        s_ref[:, c * cw:(c + 1) * cw] = jnp.dot(kb, qbd_ref[:, c * cw:(c + 1) * cw],
                                                preferred_element_type=F32)

    def stage(j, bias_idx, has_next):
        vtb = vt_ref[:, pl.ds(block_start(j), tq)]
        for c in range(n_chunks):
            cols = slice(c * cw, (c + 1) * cw)
            s = s_ref[:, cols]
            if bias_idx is not None:
                b0 = (c * cw) % tq
                s = s + bias_ref[0, bias_idx, :, b0:b0 + cw]
            m_prev = m_ref[:, cols]
            m_new = jnp.maximum(m_prev, jnp.max(s, axis=0, keepdims=True))
            p = jnp.exp2(s - m_new)
            if has_next:
                scores(j + 1, c)
            alpha = jnp.exp2(m_prev - m_new)
            acc_ref[:, cols] = alpha * acc_ref[:, cols] + jnp.dot(vtb, p.astype(BF16),
                                                                  preferred_element_type=F32)
            m_ref[:, cols] = m_new

    for c in range(n_chunks):
        scores(0, c)

    n_far = jnp.maximum(qi - 1, 0)

    def far_pair(i, carry):
        stage(2 * i, None, True)
        stage(2 * i + 1, None, True)
        return carry

    lax.fori_loop(0, n_far // 2, far_pair, 0)

    @pl.when(n_far % 2 == 1)
    def _():
        stage(n_far - 1, None, True)

    @pl.when(qi >= 1)
    def _():
        stage(qi - 1, 1, True)

    stage(qi, 0, False)

    o = acc_ref[0:hd, :] / acc_ref[hd:hd + 1, :]
    lp = lam_ref[...]
    lam = (jnp.exp(jnp.sum(lp[0:1] * lp[1:2], axis=-1, keepdims=True))
           - jnp.exp(jnp.sum(lp[2:3] * lp[3:4], axis=-1, keepdims=True)) + lambda_init)
    o = o[:, :tq] - lam * o[:, tq:]
    ms = jnp.mean(o * o, axis=0, keepdims=True)
    o = o * lax.rsqrt(ms + NORM_EPS) * sub_ref[...] * (1.0 - lambda_init)
    o_ref[...] = o.T.astype(o_ref.dtype)


def diff_attention_core(qt, k, vt, bias, lam_params, subln_w, lambda_init, tq, cw=256):
    t, width = k.shape
    hd = 2 * ATTN_HEAD_DIM
    heads = width // hd
    vrows = hd + ONES_ROWS
    assert t % tq == 0 and tq >= MAX_DISTANCE and vt.shape[0] == heads * vrows
    return pl.pallas_call(
        functools.partial(_attn_kernel, tq=tq, cw=min(cw, tq), lambda_init=lambda_init),
        name="diff_attn",
        grid=(heads, t // tq),
        in_specs=[
            pl.BlockSpec((hd, tq), lambda h, i: (h, i)),
            pl.BlockSpec((t, hd), lambda h, i: (0, h)),
            pl.BlockSpec((vrows, t), lambda h, i: (h, 0)),
            pl.BlockSpec((1, 2, tq, tq), lambda h, i: (h, 0, 0, 0)),
            pl.BlockSpec((4, ATTN_HEAD_DIM), lambda h, i: (0, 0)),
            pl.BlockSpec((hd, 1), lambda h, i: (0, 0)),
        ],
        out_specs=pl.BlockSpec((tq, hd), lambda h, i: (i, h)),
        out_shape=jax.ShapeDtypeStruct((t, width), BF16),
        scratch_shapes=[pltpu.VMEM((1, 2 * tq), F32),
                        pltpu.VMEM((vrows, 2 * tq), F32), pltpu.VMEM((hd, 2 * tq), BF16),
                        pltpu.VMEM((tq, 2 * tq), F32)],
        compiler_params=_params("parallel", "arbitrary"),
    )(qt, k, vt, bias, lam_params.astype(F32), subln_w.reshape(hd, 1).astype(F32))


def _proj_residual_kernel(x_ref, a_ref, w_ref, o_ref):
    o_ref[...] = x_ref[...] + jnp.dot(a_ref[...], w_ref[...], preferred_element_type=F32)


def proj_residual(x, a, w, tm=512):
    t, d = x.shape
    k = a.shape[1]
    tm = min(tm, t)
    return pl.pallas_call(
        _proj_residual_kernel,
        name="proj_residual",
        grid=(t // tm,),
        in_specs=[
            pl.BlockSpec((tm, d), lambda i: (i, 0)),
            pl.BlockSpec((tm, k), lambda i: (i, 0)),
            pl.BlockSpec((k, d), lambda i: (0, 0)),
        ],
        out_specs=pl.BlockSpec((tm, d), lambda i: (i, 0)),
        out_shape=jax.ShapeDtypeStruct((t, d), F32),
        compiler_params=_params("parallel"),
    )(x, a, w)


def _swiglu_step(hn, wg, wu, wd):
    gate = jnp.dot(hn, wg, preferred_element_type=F32)
    up = jnp.dot(hn, wu, preferred_element_type=F32)
    act = gate * _sigmoid(gate) * up
    return jnp.dot(act.astype(BF16), wd, preferred_element_type=F32)


def _ffn_kernel(x_ref, g_ref, wg_ref, wu_ref, wd_ref, o_ref, hn_ref, acc_ref):
    j = pl.program_id(1)

    @pl.when(j == 0)
    def _():
        x = x_ref[...]
        hn_ref[...] = _rms_rows(x, g_ref[...]).astype(BF16)
        acc_ref[...] = x

    acc_ref[...] += _swiglu_step(hn_ref[...], wg_ref[...], wu_ref[...], wd_ref[...])

    @pl.when(j == pl.num_programs(1) - 1)
    def _():
        o_ref[...] = acc_ref[...]


def ffn_residual(x, gain, w_gate, w_up, w_down, tm=1024, tf=512):
    t, d = x.shape
    f = w_gate.shape[1]
    tm = min(tm, t)
    assert t % tm == 0 and f % tf == 0
    return pl.pallas_call(
        _ffn_kernel,
        name="ffn",
        grid=(t // tm, f // tf),
        in_specs=[pl.BlockSpec((tm, d), lambda i, j: (i, 0)),
                  pl.BlockSpec((1, d), lambda i, j: (0, 0)),
                  pl.BlockSpec((d, tf), lambda i, j: (0, j)),
                  pl.BlockSpec((d, tf), lambda i, j: (0, j)),
                  pl.BlockSpec((tf, d), lambda i, j: (j, 0))],
        out_specs=pl.BlockSpec((tm, d), lambda i, j: (i, 0)),
        out_shape=jax.ShapeDtypeStruct((t, d), F32),
        scratch_shapes=[pltpu.VMEM((tm, d), BF16), pltpu.VMEM((tm, d), F32)],
        compiler_params=_params("parallel", "arbitrary"),
    )(x, gain.reshape(1, d), w_gate, w_up, w_down)


def _expert_ffn_kernel(te_ref, nu_ref, xs_ref, wg_ref, wu_ref, wd_ref, o_ref, hn_ref, acc_ref):
    i = pl.program_id(0)
    j = pl.program_id(1)

    @pl.when(i < nu_ref[0])
    def _():
        @pl.when(j == 0)
        def _():
            hn_ref[...] = xs_ref[...].astype(BF16)
            acc_ref[...] = jnp.zeros(acc_ref.shape, F32)

        acc_ref[...] += _swiglu_step(hn_ref[...], wg_ref[0], wu_ref[0], wd_ref[0])

        @pl.when(j == pl.num_programs(1) - 1)
        def _():
            o_ref[...] = acc_ref[...]

    @pl.when((i >= nu_ref[0]) & (j == 0))
    def _():
        o_ref[...] = jnp.zeros(o_ref.shape, F32)


def expert_ffn(xs, tile_expert, n_used, w_gate, w_up, w_down, tmg, tf=512):
    p, d = xs.shape
    f = w_gate.shape[2]
    nf = f // tf
    assert p % tmg == 0 and f % tf == 0

    def row_tile(i, j, te, nu):
        return (jnp.minimum(i, nu[0] - 1), 0)

    def f_tile(i, j, nu):
        return jnp.where(i < nu[0], j, nf - 1)

    return pl.pallas_call(
        _expert_ffn_kernel,
        name="expert_ffn",
        grid_spec=pltpu.PrefetchScalarGridSpec(
            num_scalar_prefetch=2,
            grid=(p // tmg, nf),
            in_specs=[pl.BlockSpec((tmg, d), row_tile),
                      pl.BlockSpec((1, d, tf), lambda i, j, te, nu: (te[i], 0, f_tile(i, j, nu))),
                      pl.BlockSpec((1, d, tf), lambda i, j, te, nu: (te[i], 0, f_tile(i, j, nu))),
                      pl.BlockSpec((1, tf, d), lambda i, j, te, nu: (te[i], f_tile(i, j, nu), 0))],
            out_specs=pl.BlockSpec((tmg, d), lambda i, j, te, nu: (i, 0)),
            scratch_shapes=[pltpu.VMEM((tmg, d), BF16), pltpu.VMEM((tmg, d), F32)],
        ),
        out_shape=jax.ShapeDtypeStruct((p, d), F32),
        compiler_params=_params("arbitrary", "arbitrary"),
    )(tile_expert, n_used, xs, w_gate, w_up, w_down)


R_E1, R_E2, R_RANK1, R_RANK2, R_W1, R_W2 = range(6)


def _router_kernel(x_ref, g_ref, r_ref, tri_ref, route_ref, cnt_ref, base_ref, *, n_experts):
    @pl.when(pl.program_id(0) == 0)
    def _():
        base_ref[...] = jnp.zeros(base_ref.shape, F32)

    hn = _rms_rows(x_ref[...], g_ref[...])
    logits = jnp.dot(hn, r_ref[...], preferred_element_type=F32, precision=lax.Precision.HIGHEST)
    lane = lax.broadcasted_iota(I32, logits.shape, 1)
    lg = jnp.where(lane < n_experts, logits, MASK_VALUE)
    m1 = jnp.max(lg, axis=-1, keepdims=True)
    i1 = jnp.min(jnp.where(lg == m1, lane, LANES), axis=-1, keepdims=True)
    lg2 = jnp.where(lane == i1, MASK_VALUE, lg)
    m2 = jnp.max(lg2, axis=-1, keepdims=True)
    i2 = jnp.min(jnp.where(lg2 == m2, lane, LANES), axis=-1, keepdims=True)
    e2 = jnp.exp(m2 - m1)
    w1 = 1.0 / (1.0 + e2)
    w2 = e2 / (1.0 + e2)

    sel1 = lane == i1
    sel2 = lane == i2
    mask = jnp.where(sel1 | sel2, 1.0, 0.0)
    rank = base_ref[...] + jnp.dot(tri_ref[...], mask.astype(BF16), preferred_element_type=F32)
    rank1 = jnp.sum(jnp.where(sel1, rank, 0.0), axis=-1, keepdims=True)
    rank2 = jnp.sum(jnp.where(sel2, rank, 0.0), axis=-1, keepdims=True)
    base_ref[...] += jnp.sum(mask, axis=0, keepdims=True)

    rec = jnp.zeros(logits.shape, F32)
    for slot, val in ((R_E1, i1.astype(F32)), (R_E2, i2.astype(F32)), (R_RANK1, rank1),
                      (R_RANK2, rank2), (R_W1, w1), (R_W2, w2)):
        rec = jnp.where(lane == slot, val, rec)
    route_ref[...] = rec
    cnt_ref[...] = jnp.broadcast_to(base_ref[...], cnt_ref.shape)


def router(x, gain, w_router, tm=512):
    t, d = x.shape
    ne = w_router.shape[1]
    tm = min(tm, t)
    r_pad = jnp.zeros((d, LANES), F32).at[:, :ne].set(w_router.astype(F32))
    idx = jnp.arange(tm)
    tri = (idx[None, :] < idx[:, None]).astype(BF16)
    return pl.pallas_call(
        functools.partial(_router_kernel, n_experts=ne),
        name="router",
        grid=(t // tm,),
        in_specs=[pl.BlockSpec((tm, d), lambda i: (i, 0)),
                  pl.BlockSpec((1, d), lambda i: (0, 0)),
                  pl.BlockSpec((d, LANES), lambda i: (0, 0)),
                  pl.BlockSpec((tm, tm), lambda i: (0, 0))],
        out_specs=[pl.BlockSpec((tm, LANES), lambda i: (i, 0)),
                   pl.BlockSpec((8, LANES), lambda i: (0, 0))],
        out_shape=[jax.ShapeDtypeStruct((t, LANES), F32),
                   jax.ShapeDtypeStruct((8, LANES), F32)],
        scratch_shapes=[pltpu.VMEM((1, LANES), F32)],
        compiler_params=_params("arbitrary"),
    )(x, gain.reshape(1, d), r_pad, tri)


def _row_copy(src, r, dst, p, sem):
    return pltpu.make_async_copy(src.at[pl.ds(r, 1), :], dst.at[pl.ds(p, 1), :], sem)


def _dispatch_kernel(pos_ref, x_ref, g_ref, xs_in_ref, xs_ref, hn_ref, sem, *, tm):
    del xs_in_ref
    base = pl.program_id(0) * tm
    hn_ref[...] = _rms_rows(x_ref[...], g_ref[...])

    def issue(r, carry):
        for k in range(TOP_K):
            _row_copy(hn_ref, r, xs_ref, pos_ref[TOP_K * (base + r) + k], sem).start()
        return carry

    lax.fori_loop(0, tm, issue, 0, unroll=8)

    def drain(r, carry):
        for k in range(TOP_K):
            _row_copy(hn_ref, r, xs_ref, pos_ref[TOP_K * (base + r) + k], sem).wait()
        return carry

    lax.fori_loop(0, tm, drain, 0, unroll=8)


def dispatch(x, gain, pos, n_rows, tm=256):
    t, d = x.shape
    tm = min(tm, t)
    zeros = jnp.zeros((n_rows, d), F32)
    return pl.pallas_call(
        functools.partial(_dispatch_kernel, tm=tm),
        name="dispatch",
        grid_spec=pltpu.PrefetchScalarGridSpec(
            num_scalar_prefetch=1,
            grid=(t // tm,),
            in_specs=[pl.BlockSpec((tm, d), lambda i, pos: (i, 0)),
                      pl.BlockSpec((1, d), lambda i, pos: (0, 0)),
                      pl.BlockSpec(memory_space=pl.ANY)],
            out_specs=pl.BlockSpec(memory_space=pl.ANY),
            scratch_shapes=[pltpu.VMEM((tm, d), F32), pltpu.SemaphoreType.DMA(())],
        ),
        out_shape=jax.ShapeDtypeStruct((n_rows, d), F32),
        input_output_aliases={3: 0},
        compiler_params=_params("arbitrary"),
    )(pos, x, gain.reshape(1, d), zeros)


def _combine_kernel(pos_ref, x_ref, route_ref, ys_ref, o_ref, buf_ref, sem, *, tm):
    base = pl.program_id(0) * tm

    def issue(r, carry):
        for k in range(TOP_K):
            _row_copy(ys_ref, pos_ref[TOP_K * (base + r) + k], buf_ref.at[k], r, sem).start()
        return carry

    lax.fori_loop(0, tm, issue, 0, unroll=8)

    def drain(r, carry):
        for k in range(TOP_K):
            _row_copy(ys_ref, pos_ref[TOP_K * (base + r) + k], buf_ref.at[k], r, sem).wait()
        return carry

    lax.fori_loop(0, tm, drain, 0, unroll=8)

    rec = route_ref[...]
    lane = lax.broadcasted_iota(I32, rec.shape, 1)
    w1 = jnp.sum(jnp.where(lane == R_W1, rec, 0.0), axis=-1, keepdims=True)
    w2 = jnp.sum(jnp.where(lane == R_W2, rec, 0.0), axis=-1, keepdims=True)
    o_ref[...] = x_ref[...] + w1 * buf_ref[0] + w2 * buf_ref[1]


def combine(x, route, ys, pos, tm=256):
    t, d = x.shape
    tm = min(tm, t)
    return pl.pallas_call(
        functools.partial(_combine_kernel, tm=tm),
        name="combine",
        grid_spec=pltpu.PrefetchScalarGridSpec(
            num_scalar_prefetch=1,
            grid=(t // tm,),
            in_specs=[pl.BlockSpec((tm, d), lambda i, pos: (i, 0)),
                      pl.BlockSpec((tm, LANES), lambda i, pos: (i, 0)),
                      pl.BlockSpec(memory_space=pl.ANY)],
            out_specs=pl.BlockSpec((tm, d), lambda i, pos: (i, 0)),
            scratch_shapes=[pltpu.VMEM((TOP_K, tm, d), F32), pltpu.SemaphoreType.DMA(())],
        ),
        out_shape=jax.ShapeDtypeStruct((t, d), F32),
        compiler_params=_params("arbitrary"),
    )(pos, x, route, ys)


def _ssd_kernel(xbc_ref, z_ref, dt_ref, cw_ref, cb_ref, dtb_ref, alog_ref, dsk_ref, nw_ref,
                o_ref, prev_ref, state_ref, y_ref, *, d_inner):
    L = SSD_CHUNK
    gn = SSD_GROUPS * SSD_D_STATE
    n_pairs = d_inner // LANES
    pairs_per_group = n_pairs // SSD_GROUPS

    @pl.when(pl.program_id(0) == 0)
    def _():
        prev_ref[...] = jnp.zeros(prev_ref.shape, F32)
        state_ref[...] = jnp.zeros(state_ref.shape, F32)

    x = xbc_ref[...]
    xp = prev_ref[...]
    row1 = lax.broadcasted_iota(I32, (L, 1), 0)
    cw = cw_ref[...]
    acc = cb_ref[...] + cw[SSD_CONV - 1:SSD_CONV] * x
    for k in range(1, SSD_CONV):
        shifted = jnp.where(row1 < k, pltpu.roll(xp, k, 0), pltpu.roll(x, k, 0))
        acc = acc + cw[SSD_CONV - 1 - k:SSD_CONV - k] * shifted
    prev_ref[...] = x
    xc = acc * _sigmoid(acc)

    dtr = dt_ref[...] + dtb_ref[...]
    dtv = jnp.maximum(dtr, 0.0) + jnp.log1p(jnp.exp(-jnp.abs(dtr)))
    da = dtv * (-jnp.exp(alog_ref[...]))
    rowh = lax.broadcasted_iota(I32, da.shape, 0)
    acs = da
    s = 1
    while s < L:
        acs = acs + jnp.where(rowh >= s, pltpu.roll(acs, s, 0), 0.0)
        s *= 2
    acs_t = acs.T
    tot = acs[L - 1:L, :]
    decay_to_end = jnp.exp(tot - acs)
    decay_from_start = jnp.exp(acs)
    tot_col = jnp.exp(acs_t[:, L - 1:L])

    lane = lax.broadcasted_iota(I32, (L, LANES), 1)
    rowl = lax.broadcasted_iota(I32, (L, LANES), 0)
    left = lane < SSD_HEAD_DIM
    causal = rowl >= lane

    def per_head(a, h0):
        return jnp.where(left, a[:, h0:h0 + 1], a[:, h0 + 1:h0 + 2])

    for g in range(SSD_GROUPS):
        bg = xc[:, d_inner + g * SSD_D_STATE:d_inner + (g + 1) * SSD_D_STATE].astype(BF16)
        cg = xc[:, d_inner + gn + g * SSD_D_STATE:d_inner + gn + (g + 1) * SSD_D_STATE].astype(BF16)
        cbm = lax.dot_general(cg, bg, (((1,), (1,)), ((), ())), preferred_element_type=F32)
        for jp in range(pairs_per_group):
            pj = g * pairs_per_group + jp
            h0 = 2 * pj
            xs_pair = xc[:, pj * LANES:(pj + 1) * LANES]
            xd = xs_pair * per_head(dtv, h0)
            y = jnp.zeros((L, LANES), F32)
            for half, hh in enumerate((h0, h0 + 1)):
                diff = acs[:, hh:hh + 1] - acs_t[hh:hh + 1, :]
                dec = jnp.exp(jnp.where(causal, diff, MASK_VALUE))
                in_half = left if half == 0 else jnp.logical_not(left)
                y = y + jnp.dot((cbm * dec).astype(BF16),
                                jnp.where(in_half, xd, 0.0).astype(BF16),
                                preferred_element_type=F32)
            s_old = state_ref[pj]
            y_off = lax.dot_general(cg, s_old.astype(BF16), (((1,), (1,)), ((), ())),
                                    preferred_element_type=F32)
            y = y + y_off * per_head(decay_from_start, h0)
            s_new = lax.dot_general((xd * per_head(decay_to_end, h0)).astype(BF16), bg,
                                    (((0,), (0,)), ((), ())), preferred_element_type=F32)
            carry = jnp.where(rowl < SSD_HEAD_DIM, tot_col[h0:h0 + 1, :], tot_col[h0 + 1:h0 + 2, :])
            state_ref[pj] = carry * s_old + s_new
            y_ref[:, pj * LANES:(pj + 1) * LANES] = y + dsk_ref[:, pj * LANES:(pj + 1) * LANES] * xs_pair

    z = z_ref[...].astype(F32)
    yg = y_ref[...] * (z * _sigmoid(z))
    gw = d_inner // SSD_GROUPS
    for g in range(SSD_GROUPS):
        blk = yg[:, g * gw:(g + 1) * gw]
        o_ref[:, g * gw:(g + 1) * gw] = _rms_rows(blk, nw_ref[:, g * gw:(g + 1) * gw]).astype(o_ref.dtype)


def ssd_core(xbc, z, dt, conv_w, conv_b, dt_bias, a_log, d_skip, norm_w):
    t = xbc.shape[0]
    d_inner = z.shape[1]
    conv_dim = xbc.shape[1]
    heads = d_inner // SSD_HEAD_DIM
    L = SSD_CHUNK
    assert t % L == 0 and heads <= LANES

    def lanes(v):
        return jnp.zeros((1, LANES), F32).at[0, :heads].set(v.astype(F32))

    row = lambda i: (i, 0)
    fixed = lambda i: (0, 0)
    return pl.pallas_call(
        functools.partial(_ssd_kernel, d_inner=d_inner),
        name="ssd",
        grid=(t // L,),
        in_specs=[
            pl.BlockSpec((L, conv_dim), row),
            pl.BlockSpec((L, d_inner), row),
            pl.BlockSpec((L, LANES), row),
            pl.BlockSpec((SSD_CONV, conv_dim), fixed),
            pl.BlockSpec((1, conv_dim), fixed),
            pl.BlockSpec((1, LANES), fixed),
            pl.BlockSpec((1, LANES), fixed),
            pl.BlockSpec((1, d_inner), fixed),
            pl.BlockSpec((1, d_inner), fixed),
        ],
        out_specs=pl.BlockSpec((L, d_inner), row),
        out_shape=jax.ShapeDtypeStruct((t, d_inner), BF16),
        scratch_shapes=[pltpu.VMEM((L, conv_dim), F32),
                        pltpu.VMEM((d_inner // LANES, LANES, SSD_D_STATE), F32),
                        pltpu.VMEM((L, d_inner), F32)],
        compiler_params=_params("arbitrary"),
    )(xbc, z, dt, conv_w.astype(F32), conv_b.reshape(1, conv_dim).astype(F32),
      lanes(dt_bias), lanes(a_log),
      jnp.repeat(d_skip.astype(F32), SSD_HEAD_DIM).reshape(1, d_inner),
      norm_w.reshape(1, d_inner).astype(F32))


def lambda_init_for(layer_idx):
    return 0.8 - 0.6 * math.exp(-0.3 * layer_idx)


def ssd_layer(xt, gain, w_in, conv_w, conv_b, dt_bias, a_log, d_skip, norm_w, w_out):
    d = xt.shape[1]
    d_inner = w_out.shape[0]
    conv_dim = conv_w.shape[1]
    heads = dt_bias.shape[0]
    w_z = w_in[:, :d_inner].astype(BF16)
    w_xbc = w_in[:, d_inner:d_inner + conv_dim].astype(BF16)
    w_dt = jnp.zeros((d, LANES), BF16).at[:, :heads].set(w_in[:, d_inner + conv_dim:].astype(BF16))
    z = norm_matmul(xt, gain, w_z, BF16)
    xbc = norm_matmul(xt, gain, w_xbc, F32)
    dt = norm_matmul(xt, gain, w_dt, F32)
    y = ssd_core(xbc, z, dt, conv_w, conv_b, dt_bias, a_log, d_skip, norm_w)
    return proj_residual(xt, y, w_out.astype(BF16))


def moe_layer(xt, gain, w_router, w_gate, w_up, w_down, tmg=512):
    t, d = xt.shape
    ne = w_router.shape[1]
    tmg = min(tmg, t)
    route, counts = router(xt, gain, w_router)
    counts = counts[0, :ne].astype(I32)
    padded = (counts + tmg - 1) // tmg * tmg
    ends = jnp.cumsum(padded)
    starts = ends - padded
    experts = route[:, R_E1:R_E2 + 1].astype(I32)
    ranks = route[:, R_RANK1:R_RANK2 + 1].astype(I32)
    pos = (jnp.take(starts, experts) + ranks).reshape(-1)
    n_rows = TOP_K * t + ne * tmg
    n_tiles = n_rows // tmg
    n_used = (ends[-1] // tmg).astype(I32)
    tile_row = jnp.minimum(jnp.arange(n_tiles, dtype=I32), n_used - 1) * tmg
    tile_expert = jnp.sum(tile_row[:, None] >= ends[None, :], axis=1).astype(I32)
    xs = dispatch(xt, gain, pos, n_rows)
    ys = expert_ffn(xs, tile_expert, n_used.reshape(1), w_gate, w_up, w_down, tmg)
    return combine(xt, route, ys, pos)


def kernel(x, rel_bias, norm_mix, norm_ffn, attn_w_in, attn_q_gain, attn_k_gain, attn_lambda, attn_subln, attn_w_out, ffn_w_gate, ffn_w_up, ffn_w_down, ssd_w_in, ssd_conv_w, ssd_conv_b, ssd_dt_bias, ssd_a_log, ssd_d, ssd_norm, ssd_w_out, moe_router, moe_w_gate, moe_w_up, moe_w_down):
    b, s, d = x.shape
    t = b * s
    assert b == 1
    depth = norm_mix.shape[0]
    xt = x.reshape(t, d).astype(F32)
    tq = min(512, t)
    bias = bias_tiles(rel_bias, tq)
    for i in range(depth):
        j = i // 2
        if i % 2 == 0:
            qt, k, vt = attn_inproj(xt, norm_mix[i], attn_w_in[j].astype(BF16), attn_q_gain[j], attn_k_gain[j])
            o = diff_attention_core(qt, k, vt, bias, attn_lambda[j], attn_subln[j], lambda_init_for(i), tq)
            xt = proj_residual(xt, o, attn_w_out[j].astype(BF16))
            xt = ffn_residual(xt, norm_ffn[i], ffn_w_gate[j].astype(BF16),
                              ffn_w_up[j].astype(BF16), ffn_w_down[j].astype(BF16))
        else:
            xt = ssd_layer(xt, norm_mix[i], ssd_w_in[j], ssd_conv_w[j], ssd_conv_b[j], ssd_dt_bias[j],
                           ssd_a_log[j], ssd_d[j], ssd_norm[j], ssd_w_out[j])
            xt = moe_layer(xt, norm_ffn[i], moe_router[j], moe_w_gate[j].astype(BF16),
                           moe_w_up[j].astype(BF16), moe_w_down[j].astype(BF16))
    return xt.reshape(b, s, d)
```

```python
import functools
import math

import jax
import jax.numpy as jnp
from jax import lax
from jax.experimental import pallas as pl
from jax.experimental.pallas import tpu as pltpu

F32 = jnp.float32
BF16 = jnp.bfloat16
I32 = jnp.int32

NORM_EPS = 1e-6
LANES = 128
SUBLANES = 8
MASK_VALUE = -1e30
LOG2E = math.log2(math.e)
ONES_ROWS = 16

ATTN_HEAD_DIM = 64
NUM_BUCKETS = 32
MAX_EXACT = NUM_BUCKETS // 2
MAX_DISTANCE = 128
SSD_HEAD_DIM = 64
SSD_GROUPS = 4
SSD_D_STATE = 128
SSD_CONV = 4
SSD_CHUNK = 128
TOP_K = 2

VMEM_LIMIT = 56 * 1024 * 1024


def _params(*sem):
    return pltpu.CompilerParams(dimension_semantics=sem, vmem_limit_bytes=VMEM_LIMIT)


def _sigmoid(x):
    return 1.0 / (1.0 + jnp.exp(-x))


def _rms_rows(x, gain):
    ms = jnp.mean(x * x, axis=-1, keepdims=True)
    return x * lax.rsqrt(ms + NORM_EPS) * gain


def _attn_inproj_kernel(x_ref, g_ref, w_ref, qkg_ref, bd_ref, qt_ref, k_ref, vt_ref, hn_ref):
    hn_ref[...] = _rms_rows(x_ref[...], g_ref[...]).astype(BF16)
    hn = hn_ref[...]
    width = k_ref.shape[1]
    bd = bd_ref[...]
    w2 = bd.shape[0]

    def head_norm(y, gain_row):
        y2 = (y * y).astype(BF16)
        ms = jnp.concatenate(
            [jnp.dot(y2[:, c * w2:(c + 1) * w2], bd, preferred_element_type=F32)
             for c in range(width // w2)], axis=1)
        return y * lax.rsqrt(ms + NORM_EPS) * gain_row

    q = jnp.dot(hn, w_ref[:, 0:width], preferred_element_type=F32)
    qt_ref[...] = head_norm(q, qkg_ref[0:1, :]).T.astype(BF16)
    k = jnp.dot(hn, w_ref[:, width:2 * width], preferred_element_type=F32)
    k_ref[...] = head_norm(k, qkg_ref[1:2, :]).astype(BF16)
    v = jnp.dot(hn, w_ref[:, 2 * width:3 * width], preferred_element_type=F32)
    vt = v.astype(BF16).T
    hd = 2 * ATTN_HEAD_DIM
    ones = jnp.ones((ONES_ROWS, vt.shape[1]), BF16)
    for h in range(width // hd):
        r0 = h * (hd + ONES_ROWS)
        vt_ref[r0:r0 + hd, :] = vt[h * hd:(h + 1) * hd, :]
        vt_ref[r0 + hd:r0 + hd + ONES_ROWS, :] = ones


def attn_inproj(x, gain, w_in, q_gain, k_gain, tm=512):
    t, d = x.shape
    width = w_in.shape[1] // 3
    tm = min(tm, t)
    reps = width // ATTN_HEAD_DIM
    qk_gain = jnp.stack([jnp.tile(q_gain, reps) * (ATTN_HEAD_DIM ** -0.5 * LOG2E),
                         jnp.tile(k_gain, reps)]).astype(F32)
    hd = 2 * ATTN_HEAD_DIM
    vt_rows = (width // hd) * (hd + ONES_ROWS)
    bw = 256
    grp = jnp.arange(bw) // ATTN_HEAD_DIM
    bd = jnp.where(grp[:, None] == grp[None, :], 1.0 / ATTN_HEAD_DIM, 0.0).astype(BF16)
    return pl.pallas_call(
        _attn_inproj_kernel,
        name="attn_inproj",
        grid=(t // tm,),
        in_specs=[
            pl.BlockSpec((tm, d), lambda i: (i, 0)),
            pl.BlockSpec((1, d), lambda i: (0, 0)),
            pl.BlockSpec((d, 3 * width), lambda i: (0, 0)),
            pl.BlockSpec((2, width), lambda i: (0, 0)),
            pl.BlockSpec((bw, bw), lambda i: (0, 0)),
        ],
        out_specs=[pl.BlockSpec((width, tm), lambda i: (0, i)),
                   pl.BlockSpec((tm, width), lambda i: (i, 0)),
                   pl.BlockSpec((vt_rows, tm), lambda i: (0, i))],
        out_shape=[jax.ShapeDtypeStruct((width, t), BF16),
                   jax.ShapeDtypeStruct((t, width), BF16),
                   jax.ShapeDtypeStruct((vt_rows, t), BF16)],
        scratch_shapes=[pltpu.VMEM((tm, d), BF16)],
        compiler_params=_params("parallel"),
    )(x, gain.reshape(1, d), w_in, qk_gain, bd)


def _bias_tiles_kernel(tab_ref, o_ref, *, tq):
    h = pl.program_id(0)
    r = lax.broadcasted_iota(I32, (tq, tq), 0)
    c = lax.broadcasted_iota(I32, (tq, tq), 1)
    far = tab_ref[NUM_BUCKETS - 1, h]
    for d in range(2):
        dist = c - r + d * tq
        n = jnp.maximum(dist, 0)
        nf = jnp.maximum(n, 1).astype(F32)
        large = MAX_EXACT + (jnp.log(nf / MAX_EXACT) / math.log(MAX_DISTANCE / MAX_EXACT)
                             * (NUM_BUCKETS - MAX_EXACT)).astype(I32)
        large = jnp.minimum(large, NUM_BUCKETS - 1)
        bucket = jnp.where(n < MAX_EXACT, n, large)
        bias = jnp.zeros((tq, tq), F32)
        for b in range(NUM_BUCKETS):
            bias = jnp.where(bucket == b, tab_ref[b, h], bias)
        o_ref[0, d] = jnp.where(dist >= 0, (bias - far) * LOG2E, MASK_VALUE)


def bias_tiles(rel_bias, tq):
    nb, h = rel_bias.shape
    return pl.pallas_call(
        functools.partial(_bias_tiles_kernel, tq=tq),
        name="bias_tiles",
        grid=(h,),
        in_specs=[pl.BlockSpec(memory_space=pltpu.SMEM)],
        out_specs=pl.BlockSpec((1, 2, tq, tq), lambda i: (i, 0, 0, 0)),
        out_shape=jax.ShapeDtypeStruct((h, 2, tq, tq), F32),
        compiler_params=_params("parallel"),
    )(rel_bias.astype(F32))


def _attn_kernel(qt_ref, k_ref, vt_ref, bias_ref, lam_ref, sub_ref, o_ref,
                 m_ref, acc_ref, qbd_ref, s_ref, *, tq, cw, lambda_init):
    qi = pl.program_id(1)
    d = ATTN_HEAD_DIM
    hd = 2 * d
    qt = qt_ref[...]
    row = lax.broadcasted_iota(I32, qt.shape, 0)
    zero = jnp.zeros_like(qt)
    qbd_ref[:, :tq] = jnp.where(row < d, qt, zero)
    qbd_ref[:, tq:] = jnp.where(row >= d, qt, zero)

    m_ref[...] = jnp.full(m_ref.shape, MASK_VALUE, F32)
    acc_ref[...] = jnp.zeros(acc_ref.shape, F32)

    n_chunks = 2 * tq // cw

    def block_start(j):
        return pl.multiple_of(j * tq, tq)

    def scores(j, c):
        kb = k_ref[pl.ds(block_start(j), tq), :]
        s_ref[:, c * cw:(c + 1) * cw] = jnp.dot(kb, qbd_ref[:, c * cw:(c + 1) * cw],
                                                preferred_element_type=F32)

    def stage(j, bias_idx, has_next):
        vtb = vt_ref[:, pl.ds(block_start(j), tq)]
        for c in range(n_chunks):
            cols = slice(c * cw, (c + 1) * cw)
            s = s_ref[:, cols]
            if bias_idx is not None:
                b0 = (c * cw) % tq
                s = s + bias_ref[0, bias_idx, :, b0:b0 + cw]
            m_prev = m_ref[:, cols]
            m_new = jnp.maximum(m_prev, jnp.max(s, axis=0, keepdims=True))
            p = jnp.exp2(s - m_new)
            if has_next:
                scores(j + 1, c)
            alpha = jnp.exp2(m_prev - m_new)
            acc_ref[:, cols] = alpha * acc_ref[:, cols] + jnp.dot(vtb, p.astype(BF16),
                                                                  preferred_element_type=F32)
            m_ref[:, cols] = m_new

    for c in range(n_chunks):
        scores(0, c)

    n_far = jnp.maximum(qi - 1, 0)

    def far_quad(i, carry):
        for u in range(4):
            stage(4 * i + u, None, True)
        return carry

    lax.fori_loop(0, n_far // 4, far_quad, 0)

    @pl.when(n_far % 4 >= 2)
    def _():
        first = n_far // 4 * 4
        stage(first, None, True)
        stage(first + 1, None, True)

    def finish(blocks):
        for idx, (j, bias_idx) in enumerate(blocks):
            stage(j, bias_idx, idx + 1 < len(blocks))
        o = acc_ref[0:hd, :] / acc_ref[hd:hd + 1, :]
        lp = lam_ref[...]
        lam = (jnp.exp(jnp.sum(lp[0:1] * lp[1:2], axis=-1, keepdims=True))
               - jnp.exp(jnp.sum(lp[2:3] * lp[3:4], axis=-1, keepdims=True)) + lambda_init)
        o = o[:, :tq] - lam * o[:, tq:]
        ms = jnp.mean(o * o, axis=0, keepdims=True)
        o = o * lax.rsqrt(ms + NORM_EPS) * sub_ref[...] * (1.0 - lambda_init)
        o_ref[...] = o.T.astype(o_ref.dtype)

    @pl.when(qi == 0)
    def _():
        finish([(qi, 0)])

    @pl.when((qi >= 1) & (n_far % 2 == 0))
    def _():
        finish([(qi - 1, 1), (qi, 0)])

    @pl.when(n_far % 2 == 1)
    def _():
        finish([(n_far - 1, None), (qi - 1, 1), (qi, 0)])


def diff_attention_core(qt, k, vt, bias, lam_params, subln_w, lambda_init, tq, cw=512):
    t, width = k.shape
    hd = 2 * ATTN_HEAD_DIM
    heads = width // hd
    vrows = hd + ONES_ROWS
    assert t % tq == 0 and tq >= MAX_DISTANCE and vt.shape[0] == heads * vrows
    return pl.pallas_call(
        functools.partial(_attn_kernel, tq=tq, cw=min(cw, tq), lambda_init=lambda_init),
        name="diff_attn",
        grid=(heads, t // tq),
        in_specs=[
            pl.BlockSpec((hd, tq), lambda h, i: (h, i)),
            pl.BlockSpec((t, hd), lambda h, i: (0, h)),
            pl.BlockSpec((vrows, t), lambda h, i: (h, 0)),
            pl.BlockSpec((1, 2, tq, tq), lambda h, i: (h, 0, 0, 0)),
            pl.BlockSpec((4, ATTN_HEAD_DIM), lambda h, i: (0, 0)),
            pl.BlockSpec((hd, 1), lambda h, i: (0, 0)),
        ],
        out_specs=pl.BlockSpec((tq, hd), lambda h, i: (i, h)),
        out_shape=jax.ShapeDtypeStruct((t, width), BF16),
        scratch_shapes=[pltpu.VMEM((1, 2 * tq), F32),
                        pltpu.VMEM((vrows, 2 * tq), F32), pltpu.VMEM((hd, 2 * tq), BF16),
                        pltpu.VMEM((tq, 2 * tq), F32)],
        compiler_params=_params("parallel", "arbitrary"),
    )(qt, k, vt, bias, lam_params.astype(F32), subln_w.reshape(hd, 1).astype(F32))


def _proj_residual_kernel(x_ref, a_ref, w_ref, o_ref):
    o_ref[...] = x_ref[...] + jnp.dot(a_ref[...], w_ref[...], preferred_element_type=F32)


def proj_residual(x, a, w, tm=512):
    t, d = x.shape
    k = a.shape[1]
    tm = min(tm, t)
    return pl.pallas_call(
        _proj_residual_kernel,
        name="proj_residual",
        grid=(t // tm,),
        in_specs=[
            pl.BlockSpec((tm, d), lambda i: (i, 0)),
            pl.BlockSpec((tm, k), lambda i: (i, 0)),
            pl.BlockSpec((k, d), lambda i: (0, 0)),
        ],
        out_specs=pl.BlockSpec((tm, d), lambda i: (i, 0)),
        out_shape=jax.ShapeDtypeStruct((t, d), F32),
        compiler_params=_params("parallel"),
    )(x, a, w)


def _swiglu_step(hn, wg, wu, wd):
    gate = jnp.dot(hn, wg, preferred_element_type=F32)
    up = jnp.dot(hn, wu, preferred_element_type=F32)
    act = gate * _sigmoid(gate) * up
    return jnp.dot(act.astype(BF16), wd, preferred_element_type=F32)


def _ffn_kernel(te_ref, first_ref, nu_ref, *refs, layer, normalize, tf):
    if normalize:
        x_ref, g_ref, wg_hbm, wu_hbm, wd_hbm, o_ref = refs[:6]
    else:
        x_ref, wg_hbm, wu_hbm, wd_hbm, o_ref = refs[:5]
    hn_ref, acc_ref, wg_res, wu_res, wd_res, wg_stg, wu_stg, wd_stg, sem = refs[-9:]
    i = pl.program_id(0)
    j = pl.program_id(1)
    nf = pl.num_programs(1)
    e = te_ref[i]

    def block_copies(jb, slot):
        f0 = pl.multiple_of(jb * tf, tf)
        return (pltpu.make_async_copy(wg_hbm.at[layer, e, :, pl.ds(f0, tf)], wg_stg.at[slot], sem.at[0, slot]),
                pltpu.make_async_copy(wu_hbm.at[layer, e, :, pl.ds(f0, tf)], wu_stg.at[slot], sem.at[1, slot]),
                pltpu.make_async_copy(wd_hbm.at[layer, e, pl.ds(f0, tf), :], wd_stg.at[slot], sem.at[2, slot]))

    @pl.when(i < nu_ref[0])
    def _():
        @pl.when(first_ref[i] == 1)
        def _():
            slot = j % 2

            @pl.when(j == 0)
            def _():
                for cp in block_copies(0, 0):
                    cp.start()

            for cp in block_copies(j, slot):
                cp.wait()

            @pl.when(j + 1 < nf)
            def _():
                for cp in block_copies(j + 1, 1 - slot):
                    cp.start()

            wg_res[j] = wg_stg[slot].astype(BF16)
            wu_res[j] = wu_stg[slot].astype(BF16)
            wd_res[j] = wd_stg[slot].astype(BF16)

        @pl.when(j == 0)
        def _():
            x = x_ref[...]
            if normalize:
                hn_ref[...] = _rms_rows(x, g_ref[...]).astype(BF16)
                acc_ref[...] = x
            else:
                hn_ref[...] = x.astype(BF16)
                acc_ref[...] = jnp.zeros(acc_ref.shape, F32)

        acc_ref[...] += _swiglu_step(hn_ref[...], wg_res[j], wu_res[j], wd_res[j])

        @pl.when(j == nf - 1)
        def _():
            o_ref[...] = acc_ref[...]

    @pl.when((i >= nu_ref[0]) & (j == 0))
    def _():
        o_ref[...] = jnp.zeros(o_ref.shape, F32)


def ffn_tiles(x, tile_expert, first, n_used, w_gate, w_up, w_down, layer, gain=None, tm=512, tf=512):
    rows, d = x.shape
    f = w_gate.shape[3]
    nf = f // tf
    assert rows % tm == 0 and f % tf == 0
    normalize = gain is not None

    def row_tile(i, j, te, fr, nu):
        return (jnp.minimum(i, nu[0] - 1), 0)

    in_specs = [pl.BlockSpec((tm, d), row_tile)]
    args = [x]
    if normalize:
        in_specs.append(pl.BlockSpec((1, d), lambda i, j, te, fr, nu: (0, 0)))
        args.append(gain.reshape(1, d))
    in_specs += [pl.BlockSpec(memory_space=pl.ANY)] * 3
    return pl.pallas_call(
        functools.partial(_ffn_kernel, layer=layer, normalize=normalize, tf=tf),
        name="ffn",
        grid_spec=pltpu.PrefetchScalarGridSpec(
            num_scalar_prefetch=3,
            grid=(rows // tm, nf),
            in_specs=in_specs,
            out_specs=pl.BlockSpec((tm, d), lambda i, j, te, fr, nu: (i, 0)),
            scratch_shapes=[pltpu.VMEM((tm, d), BF16), pltpu.VMEM((tm, d), F32),
                            pltpu.VMEM((nf, d, tf), BF16), pltpu.VMEM((nf, d, tf), BF16),
                            pltpu.VMEM((nf, tf, d), BF16),
                            pltpu.VMEM((2, d, tf), F32), pltpu.VMEM((2, d, tf), F32),
                            pltpu.VMEM((2, tf, d), F32),
                            pltpu.SemaphoreType.DMA((3, 2))],
        ),
        out_shape=jax.ShapeDtypeStruct((rows, d), F32),
        compiler_params=_params("arbitrary", "arbitrary"),
    )(tile_expert, first, n_used, *args, w_gate, w_up, w_down)


def ffn_residual(x, gain, w_gate, w_up, w_down, layer, tm=512):
    n_tiles = x.shape[0] // min(tm, x.shape[0])
    tiles = jnp.arange(n_tiles, dtype=I32)
    return ffn_tiles(x, jnp.zeros((n_tiles,), I32), (tiles == 0).astype(I32),
                     jnp.full((1,), n_tiles, I32), w_gate[:, None], w_up[:, None], w_down[:, None],
                     layer, gain=gain, tm=min(tm, x.shape[0]))


R_E1, R_E2, R_RANK1, R_RANK2, R_W1, R_W2 = range(6)


def _router_kernel(x_ref, g_ref, r_ref, tri_ref, route_ref, cnt_ref, base_ref, *, n_experts):
    @pl.when(pl.program_id(0) == 0)
    def _():
        base_ref[...] = jnp.zeros(base_ref.shape, F32)

    hn = _rms_rows(x_ref[...], g_ref[...])
    logits = jnp.dot(hn, r_ref[...], preferred_element_type=F32, precision=lax.Precision.HIGHEST)
    lane = lax.broadcasted_iota(I32, logits.shape, 1)
    lg = jnp.where(lane < n_experts, logits, MASK_VALUE)
    m1 = jnp.max(lg, axis=-1, keepdims=True)
    i1 = jnp.min(jnp.where(lg == m1, lane, LANES), axis=-1, keepdims=True)
    lg2 = jnp.where(lane == i1, MASK_VALUE, lg)
    m2 = jnp.max(lg2, axis=-1, keepdims=True)
    i2 = jnp.min(jnp.where(lg2 == m2, lane, LANES), axis=-1, keepdims=True)
    e2 = jnp.exp(m2 - m1)
    w1 = 1.0 / (1.0 + e2)
    w2 = e2 / (1.0 + e2)

    sel1 = lane == i1
    sel2 = lane == i2
    mask = jnp.where(sel1 | sel2, 1.0, 0.0)
    rank = base_ref[...] + jnp.dot(tri_ref[...], mask.astype(BF16), preferred_element_type=F32)
    rank1 = jnp.sum(jnp.where(sel1, rank, 0.0), axis=-1, keepdims=True)
    rank2 = jnp.sum(jnp.where(sel2, rank, 0.0), axis=-1, keepdims=True)
    base_ref[...] += jnp.sum(mask, axis=0, keepdims=True)

    rec = jnp.zeros(logits.shape, F32)
    for slot, val in ((R_E1, i1.astype(F32)), (R_E2, i2.astype(F32)), (R_RANK1, rank1),
                      (R_RANK2, rank2), (R_W1, w1), (R_W2, w2)):
        rec = jnp.where(lane == slot, val, rec)
    route_ref[...] = rec
    cnt_ref[...] = jnp.broadcast_to(base_ref[...], cnt_ref.shape)


def router(x, gain, w_router, tm=512):
    t, d = x.shape
    ne = w_router.shape[1]
    tm = min(tm, t)
    r_pad = jnp.zeros((d, LANES), F32).at[:, :ne].set(w_router.astype(F32))
    idx = jnp.arange(tm)
    tri = (idx[None, :] < idx[:, None]).astype(BF16)
    return pl.pallas_call(
        functools.partial(_router_kernel, n_experts=ne),
        name="router",
        grid=(t // tm,),
        in_specs=[pl.BlockSpec((tm, d), lambda i: (i, 0)),
                  pl.BlockSpec((1, d), lambda i: (0, 0)),
                  pl.BlockSpec((d, LANES), lambda i: (0, 0)),
                  pl.BlockSpec((tm, tm), lambda i: (0, 0))],
        out_specs=[pl.BlockSpec((tm, LANES), lambda i: (i, 0)),
                   pl.BlockSpec((8, LANES), lambda i: (0, 0))],
        out_shape=[jax.ShapeDtypeStruct((t, LANES), F32),
                   jax.ShapeDtypeStruct((8, LANES), F32)],
        scratch_shapes=[pltpu.VMEM((1, LANES), F32)],
        compiler_params=_params("arbitrary"),
    )(x, gain.reshape(1, d), r_pad, tri)


def _row_copy(src, r, dst, p, sem):
    return pltpu.make_async_copy(src.at[pl.ds(r, 1), :], dst.at[pl.ds(p, 1), :], sem)


def _dispatch_kernel(pos_ref, x_ref, g_ref, xs_in_ref, xs_ref, hn_ref, sem, *, tm):
    del xs_in_ref
    base = pl.program_id(0) * tm
    hn_ref[...] = _rms_rows(x_ref[...], g_ref[...])

    def issue(r, carry):
        for k in range(TOP_K):
            _row_copy(hn_ref, r, xs_ref, pos_ref[TOP_K * (base + r) + k], sem).start()
        return carry

    lax.fori_loop(0, tm, issue, 0, unroll=8)

    def drain(r, carry):
        for k in range(TOP_K):
            _row_copy(hn_ref, r, xs_ref, pos_ref[TOP_K * (base + r) + k], sem).wait()
        return carry

    lax.fori_loop(0, tm, drain, 0, unroll=8)


def dispatch(x, gain, pos, n_rows, tm=256):
    t, d = x.shape
    tm = min(tm, t)
    zeros = jnp.zeros((n_rows, d), F32)
    return pl.pallas_call(
        functools.partial(_dispatch_kernel, tm=tm),
        name="dispatch",
        grid_spec=pltpu.PrefetchScalarGridSpec(
            num_scalar_prefetch=1,
            grid=(t // tm,),
            in_specs=[pl.BlockSpec((tm, d), lambda i, pos: (i, 0)),
                      pl.BlockSpec((1, d), lambda i, pos: (0, 0)),
                      pl.BlockSpec(memory_space=pl.ANY)],
            out_specs=pl.BlockSpec(memory_space=pl.ANY),
            scratch_shapes=[pltpu.VMEM((tm, d), F32), pltpu.SemaphoreType.DMA(())],
        ),
        out_shape=jax.ShapeDtypeStruct((n_rows, d), F32),
        input_output_aliases={3: 0},
        compiler_params=_params("arbitrary"),
    )(pos, x, gain.reshape(1, d), zeros)


def _combine_kernel(pos_ref, x_ref, route_ref, ys_ref, o_ref, buf_ref, sem, *, tm):
    base = pl.program_id(0) * tm

    def issue(r, carry):
        for k in range(TOP_K):
            _row_copy(ys_ref, pos_ref[TOP_K * (base + r) + k], buf_ref.at[k], r, sem).start()
        return carry

    lax.fori_loop(0, tm, issue, 0, unroll=8)

    def drain(r, carry):
        for k in range(TOP_K):
            _row_copy(ys_ref, pos_ref[TOP_K * (base + r) + k], buf_ref.at[k], r, sem).wait()
        return carry

    lax.fori_loop(0, tm, drain, 0, unroll=8)

    rec = route_ref[...]
    lane = lax.broadcasted_iota(I32, rec.shape, 1)
    w1 = jnp.sum(jnp.where(lane == R_W1, rec, 0.0), axis=-1, keepdims=True)
    w2 = jnp.sum(jnp.where(lane == R_W2, rec, 0.0), axis=-1, keepdims=True)
    o_ref[...] = x_ref[...] + w1 * buf_ref[0] + w2 * buf_ref[1]


def combine(x, route, ys, pos, tm=256):
    t, d = x.shape
    tm = min(tm, t)
    return pl.pallas_call(
        functools.partial(_combine_kernel, tm=tm),
        name="combine",
        grid_spec=pltpu.PrefetchScalarGridSpec(
            num_scalar_prefetch=1,
            grid=(t // tm,),
            in_specs=[pl.BlockSpec((tm, d), lambda i, pos: (i, 0)),
                      pl.BlockSpec((tm, LANES), lambda i, pos: (i, 0)),
                      pl.BlockSpec(memory_space=pl.ANY)],
            out_specs=pl.BlockSpec((tm, d), lambda i, pos: (i, 0)),
            scratch_shapes=[pltpu.VMEM((TOP_K, tm, d), F32), pltpu.SemaphoreType.DMA(())],
        ),
        out_shape=jax.ShapeDtypeStruct((t, d), F32),
        compiler_params=_params("arbitrary"),
    )(pos, x, route, ys)


def _ssd_kernel(xbc_ref, z_ref, dt_ref, cw_ref, cb_ref, dtb_ref, alog_ref, dsk_ref, nw_ref,
                o_ref, prev_ref, state_ref, y_ref, *, d_inner):
    L = SSD_CHUNK
    gn = SSD_GROUPS * SSD_D_STATE
    n_pairs = d_inner // LANES
    pairs_per_group = n_pairs // SSD_GROUPS

    @pl.when(pl.program_id(0) == 0)
    def _():
        prev_ref[...] = jnp.zeros(prev_ref.shape, F32)
        state_ref[...] = jnp.zeros(state_ref.shape, F32)

    x = xbc_ref[...]
    xp = prev_ref[...]
    row1 = lax.broadcasted_iota(I32, (SUBLANES, 1), 0)
    cw = cw_ref[...]
    acc = cb_ref[...] + cw[SSD_CONV - 1:SSD_CONV] * x
    for k in range(1, SSD_CONV):
        shifted = pltpu.roll(x, k, 0)
        head = jnp.where(row1 < k, pltpu.roll(xp, k, 0), shifted[:SUBLANES])
        shifted = jnp.concatenate([head, shifted[SUBLANES:]], axis=0)
        acc = acc + cw[SSD_CONV - 1 - k:SSD_CONV - k] * shifted
    prev_ref[...] = x[L - SUBLANES:]
    xc = acc * _sigmoid(acc)

    dtr = dt_ref[...] + dtb_ref[...]
    dtv = jnp.maximum(dtr, 0.0) + jnp.log1p(jnp.exp(-jnp.abs(dtr)))
    da = dtv * (-jnp.exp(alog_ref[...]))
    rowh = lax.broadcasted_iota(I32, da.shape, 0)
    acs = da
    s = 1
    while s < L:
        acs = acs + jnp.where(rowh >= s, pltpu.roll(acs, s, 0), 0.0)
        s *= 2
    acs_t = acs.T
    tot = acs[L - 1:L, :]
    decay_to_end = jnp.exp(tot - acs)
    decay_from_start = jnp.exp(acs)
    tot_col = jnp.exp(acs_t[:, L - 1:L])

    lane = lax.broadcasted_iota(I32, (L, LANES), 1)
    rowl = lax.broadcasted_iota(I32, (L, LANES), 0)
    left = lane < SSD_HEAD_DIM
    causal = rowl >= lane

    def per_head(a, h0):
        return jnp.where(left, a[:, h0:h0 + 1], a[:, h0 + 1:h0 + 2])

    for g in range(SSD_GROUPS):
        bg = xc[:, d_inner + g * SSD_D_STATE:d_inner + (g + 1) * SSD_D_STATE].astype(BF16)
        cg = xc[:, d_inner + gn + g * SSD_D_STATE:d_inner + gn + (g + 1) * SSD_D_STATE].astype(BF16)
        cbm = lax.dot_general(cg, bg, (((1,), (1,)), ((), ())), preferred_element_type=F32)
        for jp in range(pairs_per_group):
            pj = g * pairs_per_group + jp
            h0 = 2 * pj
            xs_pair = xc[:, pj * LANES:(pj + 1) * LANES]
            xd = xs_pair * per_head(dtv, h0)
            y = jnp.zeros((L, LANES), F32)
            for half, hh in enumerate((h0, h0 + 1)):
                diff = acs[:, hh:hh + 1] - acs_t[hh:hh + 1, :]
                dec = jnp.exp(jnp.where(causal, diff, MASK_VALUE))
                in_half = left if half == 0 else jnp.logical_not(left)
                y = y + jnp.dot((cbm * dec).astype(BF16),
                                jnp.where(in_half, xd, 0.0).astype(BF16),
                                preferred_element_type=F32)
            s_old = state_ref[pj]
            y_off = lax.dot_general(cg, s_old.astype(BF16), (((1,), (1,)), ((), ())),
                                    preferred_element_type=F32)
            y = y + y_off * per_head(decay_from_start, h0)
            s_new = lax.dot_general((xd * per_head(decay_to_end, h0)).astype(BF16), bg,
                                    (((0,), (0,)), ((), ())), preferred_element_type=F32)
            carry = jnp.where(rowl < SSD_HEAD_DIM, tot_col[h0:h0 + 1, :], tot_col[h0 + 1:h0 + 2, :])
            state_ref[pj] = carry * s_old + s_new
            y_ref[:, pj * LANES:(pj + 1) * LANES] = y + dsk_ref[:, pj * LANES:(pj + 1) * LANES] * xs_pair

    z = z_ref[...].astype(F32)
    yg = y_ref[...] * (z * _sigmoid(z))
    gw = d_inner // SSD_GROUPS
    for g in range(SSD_GROUPS):
        blk = yg[:, g * gw:(g + 1) * gw]
        o_ref[:, g * gw:(g + 1) * gw] = _rms_rows(blk, nw_ref[:, g * gw:(g + 1) * gw]).astype(o_ref.dtype)


def ssd_core(xbc, z, dt, conv_w, conv_b, dt_bias, a_log, d_skip, norm_w):
    t = xbc.shape[0]
    d_inner = z.shape[1]
    conv_dim = xbc.shape[1]
    heads = d_inner // SSD_HEAD_DIM
    L = SSD_CHUNK
    assert t % L == 0 and heads <= LANES

    def lanes(v):
        return jnp.zeros((1, LANES), F32).at[0, :heads].set(v.astype(F32))

    row = lambda i: (i, 0)
    fixed = lambda i: (0, 0)
    return pl.pallas_call(
        functools.partial(_ssd_kernel, d_inner=d_inner),
        name="ssd",
        grid=(t // L,),
        in_specs=[
            pl.BlockSpec((L, conv_dim), row),
            pl.BlockSpec((L, d_inner), row),
            pl.BlockSpec((L, LANES), row),
            pl.BlockSpec((SSD_CONV, conv_dim), fixed),
            pl.BlockSpec((1, conv_dim), fixed),
            pl.BlockSpec((1, LANES), fixed),
            pl.BlockSpec((1, LANES), fixed),
            pl.BlockSpec((1, d_inner), fixed),
            pl.BlockSpec((1, d_inner), fixed),
        ],
        out_specs=pl.BlockSpec((L, d_inner), row),
        out_shape=jax.ShapeDtypeStruct((t, d_inner), BF16),
        scratch_shapes=[pltpu.VMEM((SUBLANES, conv_dim), F32),
                        pltpu.VMEM((d_inner // LANES, LANES, SSD_D_STATE), F32),
                        pltpu.VMEM((L, d_inner), F32)],
        compiler_params=_params("arbitrary"),
    )(xbc, z, dt, conv_w.astype(F32), conv_b.reshape(1, conv_dim).astype(F32),
      lanes(dt_bias), lanes(a_log),
      jnp.repeat(d_skip.astype(F32), SSD_HEAD_DIM).reshape(1, d_inner),
      norm_w.reshape(1, d_inner).astype(F32))


def lambda_init_for(layer_idx):
    return 0.8 - 0.6 * math.exp(-0.3 * layer_idx)


def _ssd_inproj_kernel(x_ref, g_ref, w_hbm, wdt_ref, z_ref, xbc_ref, dt_ref,
                       hn_ref, wres_ref, stg_ref, sem, *, layer, tn, d_inner):
    nb = wres_ref.shape[0]

    def block_copy(jb, slot):
        c0 = pl.multiple_of(jb * tn, tn)
        return pltpu.make_async_copy(w_hbm.at[layer, :, pl.ds(c0, tn)], stg_ref.at[slot], sem.at[slot])

    @pl.when(pl.program_id(0) == 0)
    def _():
        block_copy(0, 0).start()

        def body(jb, carry):
            slot = jb % 2
            block_copy(jb, slot).wait()

            @pl.when(jb + 1 < nb)
            def _():
                block_copy(jb + 1, 1 - slot).start()

            wres_ref[jb] = stg_ref[slot].astype(BF16)
            return carry

        lax.fori_loop(0, nb, body, 0)

    hn_ref[...] = _rms_rows(x_ref[...], g_ref[...]).astype(BF16)
    hn = hn_ref[...]
    for jb in range(nb):
        y = jnp.dot(hn, wres_ref[jb], preferred_element_type=F32)
        c0 = jb * tn
        if c0 < d_inner:
            z_ref[:, c0:c0 + tn] = y.astype(z_ref.dtype)
        else:
            xbc_ref[:, c0 - d_inner:c0 - d_inner + tn] = y
    dt_ref[...] = jnp.dot(hn, wdt_ref[...], preferred_element_type=F32)


def ssd_inproj(x, gain, w_in_all, layer, d_inner, conv_dim, heads, tm=512, tn=512):
    t, d = x.shape
    tm = min(tm, t)
    n_main = d_inner + conv_dim
    assert t % tm == 0 and d_inner % tn == 0 and conv_dim % tn == 0 and heads <= LANES
    w_dt = jnp.zeros((d, LANES), BF16).at[:, :heads].set(w_in_all[layer, :, n_main:].astype(BF16))
    row = lambda i: (i, 0)
    fixed = lambda i: (0, 0)
    return pl.pallas_call(
        functools.partial(_ssd_inproj_kernel, layer=layer, tn=tn, d_inner=d_inner),
        name="ssd_inproj",
        grid=(t // tm,),
        in_specs=[pl.BlockSpec((tm, d), row),
                  pl.BlockSpec((1, d), fixed),
                  pl.BlockSpec(memory_space=pl.ANY),
                  pl.BlockSpec((d, LANES), fixed)],
        out_specs=[pl.BlockSpec((tm, d_inner), row),
                   pl.BlockSpec((tm, conv_dim), row),
                   pl.BlockSpec((tm, LANES), row)],
        out_shape=[jax.ShapeDtypeStruct((t, d_inner), BF16),
                   jax.ShapeDtypeStruct((t, conv_dim), F32),
                   jax.ShapeDtypeStruct((t, LANES), F32)],
        scratch_shapes=[pltpu.VMEM((tm, d), BF16),
                        pltpu.VMEM((n_main // tn, d, tn), BF16),
                        pltpu.VMEM((2, d, tn), F32),
                        pltpu.SemaphoreType.DMA((2,))],
        compiler_params=_params("arbitrary"),
    )(x, gain.reshape(1, d), w_in_all, w_dt)


def ssd_layer(xt, gain, w_in_all, layer, conv_w, conv_b, dt_bias, a_log, d_skip, norm_w, w_out):
    d_inner = w_out.shape[0]
    conv_dim = conv_w.shape[1]
    heads = dt_bias.shape[0]
    z, xbc, dt = ssd_inproj(xt, gain, w_in_all, layer, d_inner, conv_dim, heads)
    y = ssd_core(xbc, z, dt, conv_w, conv_b, dt_bias, a_log, d_skip, norm_w)
    return proj_residual(xt, y, w_out.astype(BF16))


def moe_layer(xt, gain, w_router, w_gate, w_up, w_down, layer, tmg=512):
    t, d = xt.shape
    ne = w_router.shape[1]
    tmg = min(tmg, t)
    route, counts = router(xt, gain, w_router)
    counts = counts[0, :ne].astype(I32)
    padded = (counts + tmg - 1) // tmg * tmg
    ends = jnp.cumsum(padded)
    starts = ends - padded
    experts = route[:, R_E1:R_E2 + 1].astype(I32)
    ranks = route[:, R_RANK1:R_RANK2 + 1].astype(I32)
    pos = (jnp.take(starts, experts) + ranks).reshape(-1)
    n_rows = TOP_K * t + ne * tmg
    n_tiles = n_rows // tmg
    n_used = (ends[-1] // tmg).astype(I32)
    tile_row = jnp.minimum(jnp.arange(n_tiles, dtype=I32), n_used - 1) * tmg
    tile_expert = jnp.sum(tile_row[:, None] >= ends[None, :], axis=1).astype(I32)
    first = jnp.concatenate([jnp.ones((1,), I32), (tile_expert[1:] != tile_expert[:-1]).astype(I32)])
    xs = dispatch(xt, gain, pos, n_rows)
    ys = ffn_tiles(xs, tile_expert, first, n_used.reshape(1), w_gate, w_up, w_down, layer, tm=tmg)
    return combine(xt, route, ys, pos)


def kernel(x, rel_bias, norm_mix, norm_ffn, attn_w_in, attn_q_gain, attn_k_gain, attn_lambda, attn_subln, attn_w_out, ffn_w_gate, ffn_w_up, ffn_w_down, ssd_w_in, ssd_conv_w, ssd_conv_b, ssd_dt_bias, ssd_a_log, ssd_d, ssd_norm, ssd_w_out, moe_router, moe_w_gate, moe_w_up, moe_w_down):
    b, s, d = x.shape
    t = b * s
    assert b == 1
    depth = norm_mix.shape[0]
    xt = x.reshape(t, d).astype(F32)
    tq = min(512, t)
    bias = bias_tiles(rel_bias, tq)
    for i in range(depth):
        j = i // 2
        if i % 2 == 0:
            qt, k, vt = attn_inproj(xt, norm_mix[i], attn_w_in[j].astype(BF16), attn_q_gain[j], attn_k_gain[j])
            o = diff_attention_core(qt, k, vt, bias, attn_lambda[j], attn_subln[j], lambda_init_for(i), tq)
            xt = proj_residual(xt, o, attn_w_out[j].astype(BF16))
            xt = ffn_residual(xt, norm_ffn[i], ffn_w_gate, ffn_w_up, ffn_w_down, j)
        else:
            xt = ssd_layer(xt, norm_mix[i], ssd_w_in, j, ssd_conv_w[j], ssd_conv_b[j], ssd_dt_bias[j],
                           ssd_a_log[j], ssd_d[j], ssd_norm[j], ssd_w_out[j])
            xt = moe_layer(xt, norm_ffn[i], moe_router[j], moe_w_gate, moe_w_up, moe_w_down, j)
    return xt.reshape(b, s, d)
```

```python
import functools
import math

import jax
import jax.numpy as jnp
from jax import lax
from jax.experimental import pallas as pl
from jax.experimental.pallas import tpu as pltpu

F32 = jnp.float32
BF16 = jnp.bfloat16
I32 = jnp.int32

NORM_EPS = 1e-6
LANES = 128
SUBLANES = 8
MASK_VALUE = -1e30
LOG2E = math.log2(math.e)
ONES_ROWS = 16

ATTN_HEAD_DIM = 64
NUM_BUCKETS = 32
MAX_EXACT = NUM_BUCKETS // 2
MAX_DISTANCE = 128
SSD_HEAD_DIM = 64
SSD_GROUPS = 4
SSD_D_STATE = 128
SSD_CONV = 4
SSD_CHUNK = 128
TOP_K = 2

VMEM_LIMIT = 56 * 1024 * 1024


def _params(*sem):
    return pltpu.CompilerParams(dimension_semantics=sem, vmem_limit_bytes=VMEM_LIMIT)


def _sigmoid(x):
    return 1.0 / (1.0 + jnp.exp(-x))


def _rms_rows(x, gain):
    ms = jnp.mean(x * x, axis=-1, keepdims=True)
    return x * lax.rsqrt(ms + NORM_EPS) * gain


def _attn_inproj_kernel(x_ref, g_ref, w_ref, qkg_ref, bd_ref, qt_ref, k_ref, vt_ref, hn_ref):
    hn_ref[...] = _rms_rows(x_ref[...], g_ref[...]).astype(BF16)
    hn = hn_ref[...]
    width = k_ref.shape[1]
    bd = bd_ref[...]
    w2 = bd.shape[0]

    def head_norm(y, gain_row):
        y2 = (y * y).astype(BF16)
        ms = jnp.concatenate(
            [jnp.dot(y2[:, c * w2:(c + 1) * w2], bd, preferred_element_type=F32)
             for c in range(width // w2)], axis=1)
        return y * lax.rsqrt(ms + NORM_EPS) * gain_row

    q = jnp.dot(hn, w_ref[:, 0:width], preferred_element_type=F32)
    qt_ref[...] = head_norm(q, qkg_ref[0:1, :]).T.astype(BF16)
    k = jnp.dot(hn, w_ref[:, width:2 * width], preferred_element_type=F32)
    k_ref[...] = head_norm(k, qkg_ref[1:2, :]).astype(BF16)
    v = jnp.dot(hn, w_ref[:, 2 * width:3 * width], preferred_element_type=F32)
    vt = v.astype(BF16).T
    hd = 2 * ATTN_HEAD_DIM
    ones = jnp.ones((ONES_ROWS, vt.shape[1]), BF16)
    for h in range(width // hd):
        r0 = h * (hd + ONES_ROWS)
        vt_ref[r0:r0 + hd, :] = vt[h * hd:(h + 1) * hd, :]
        vt_ref[r0 + hd:r0 + hd + ONES_ROWS, :] = ones


def attn_inproj(x, gain, w_in, q_gain, k_gain, tm=512):
    t, d = x.shape
    width = w_in.shape[1] // 3
    tm = min(tm, t)
    reps = width // ATTN_HEAD_DIM
    qk_gain = jnp.stack([jnp.tile(q_gain, reps) * (ATTN_HEAD_DIM ** -0.5 * LOG2E),
                         jnp.tile(k_gain, reps)]).astype(F32)
    hd = 2 * ATTN_HEAD_DIM
    vt_rows = (width // hd) * (hd + ONES_ROWS)
    bw = 256
    grp = jnp.arange(bw) // ATTN_HEAD_DIM
    bd = jnp.where(grp[:, None] == grp[None, :], 1.0 / ATTN_HEAD_DIM, 0.0).astype(BF16)
    return pl.pallas_call(
        _attn_inproj_kernel,
        name="attn_inproj",
        grid=(t // tm,),
        in_specs=[
            pl.BlockSpec((tm, d), lambda i: (i, 0)),
            pl.BlockSpec((1, d), lambda i: (0, 0)),
            pl.BlockSpec((d, 3 * width), lambda i: (0, 0)),
            pl.BlockSpec((2, width), lambda i: (0, 0)),
            pl.BlockSpec((bw, bw), lambda i: (0, 0)),
        ],
        out_specs=[pl.BlockSpec((width, tm), lambda i: (0, i)),
                   pl.BlockSpec((tm, width), lambda i: (i, 0)),
                   pl.BlockSpec((vt_rows, tm), lambda i: (0, i))],
        out_shape=[jax.ShapeDtypeStruct((width, t), BF16),
                   jax.ShapeDtypeStruct((t, width), BF16),
                   jax.ShapeDtypeStruct((vt_rows, t), BF16)],
        scratch_shapes=[pltpu.VMEM((tm, d), BF16)],
        compiler_params=_params("parallel"),
    )(x, gain.reshape(1, d), w_in, qk_gain, bd)


def _bias_tiles_kernel(tab_ref, o_ref, *, tq):
    h = pl.program_id(0)
    r = lax.broadcasted_iota(I32, (tq, tq), 0)
    c = lax.broadcasted_iota(I32, (tq, tq), 1)
    far = tab_ref[NUM_BUCKETS - 1, h]
    for d in range(2):
        dist = c - r + d * tq
        n = jnp.maximum(dist, 0)
        nf = jnp.maximum(n, 1).astype(F32)
        large = MAX_EXACT + (jnp.log(nf / MAX_EXACT) / math.log(MAX_DISTANCE / MAX_EXACT)
                             * (NUM_BUCKETS - MAX_EXACT)).astype(I32)
        large = jnp.minimum(large, NUM_BUCKETS - 1)
        bucket = jnp.where(n < MAX_EXACT, n, large)
        bias = jnp.zeros((tq, tq), F32)
        for b in range(NUM_BUCKETS):
            bias = jnp.where(bucket == b, tab_ref[b, h], bias)
        o_ref[0, d] = jnp.where(dist >= 0, (bias - far) * LOG2E, MASK_VALUE)


def bias_tiles(rel_bias, tq):
    nb, h = rel_bias.shape
    return pl.pallas_call(
        functools.partial(_bias_tiles_kernel, tq=tq),
        name="bias_tiles",
        grid=(h,),
        in_specs=[pl.BlockSpec(memory_space=pltpu.SMEM)],
        out_specs=pl.BlockSpec((1, 2, tq, tq), lambda i: (i, 0, 0, 0)),
        out_shape=jax.ShapeDtypeStruct((h, 2, tq, tq), F32),
        compiler_params=_params("parallel"),
    )(rel_bias.astype(F32))


def _attn_kernel(qt_ref, k_ref, vt_ref, bias_ref, lam_ref, sub_ref, o_ref,
                 m_ref, smax_ref, acc_ref, qbd_ref, s_ref, *, tq, cw, lambda_init):
    qi = pl.program_id(1)
    d = ATTN_HEAD_DIM
    hd = 2 * d
    qt = qt_ref[...]
    row = lax.broadcasted_iota(I32, qt.shape, 0)
    zero = jnp.zeros_like(qt)
    qbd_ref[:, :tq] = jnp.where(row < d, qt, zero)
    qbd_ref[:, tq:] = jnp.where(row >= d, qt, zero)

    m_ref[...] = jnp.full(m_ref.shape, MASK_VALUE, F32)
    acc_ref[...] = jnp.zeros(acc_ref.shape, F32)

    n_chunks = 2 * tq // cw

    def block_start(j):
        return pl.multiple_of(j * tq, tq)

    def scores(j, c):
        kb = k_ref[pl.ds(block_start(j), tq), :]
        s = jnp.dot(kb, qbd_ref[:, c * cw:(c + 1) * cw], preferred_element_type=F32)
        s_ref[:, c * cw:(c + 1) * cw] = s
        smax_ref[:, c * cw:(c + 1) * cw] = jnp.max(s, axis=0, keepdims=True)

    def stage(j, bias_idx, has_next):
        vtb = vt_ref[:, pl.ds(block_start(j), tq)]
        for c in range(n_chunks):
            cols = slice(c * cw, (c + 1) * cw)
            s = s_ref[:, cols]
            if bias_idx is None:
                s_max = smax_ref[:, cols]
            else:
                b0 = (c * cw) % tq
                s = s + bias_ref[0, bias_idx, :, b0:b0 + cw]
                s_max = jnp.max(s, axis=0, keepdims=True)
            m_prev = m_ref[:, cols]
            m_new = jnp.maximum(m_prev, s_max)
            p = jnp.exp2(s - m_new)
            if has_next:
                scores(j + 1, c)
            alpha = jnp.exp2(m_prev - m_new)
            acc_ref[:, cols] = alpha * acc_ref[:, cols] + jnp.dot(vtb, p.astype(BF16),
                                                                  preferred_element_type=F32)
            m_ref[:, cols] = m_new

    for c in range(n_chunks):
        scores(0, c)

    n_far = jnp.maximum(qi - 1, 0)

    def far_quad(i, carry):
        for u in range(4):
            stage(4 * i + u, None, True)
        return carry

    lax.fori_loop(0, n_far // 4, far_quad, 0)

    @pl.when(n_far % 4 >= 2)
    def _():
        first = n_far // 4 * 4
        stage(first, None, True)
        stage(first + 1, None, True)

    def finish(blocks):
        for idx, (j, bias_idx) in enumerate(blocks):
            stage(j, bias_idx, idx + 1 < len(blocks))
        o = acc_ref[0:hd, :] / acc_ref[hd:hd + 1, :]
        lp = lam_ref[...]
        lam = (jnp.exp(jnp.sum(lp[0:1] * lp[1:2], axis=-1, keepdims=True))
               - jnp.exp(jnp.sum(lp[2:3] * lp[3:4], axis=-1, keepdims=True)) + lambda_init)
        o = o[:, :tq] - lam * o[:, tq:]
        ms = jnp.mean(o * o, axis=0, keepdims=True)
        o = o * lax.rsqrt(ms + NORM_EPS) * sub_ref[...] * (1.0 - lambda_init)
        o_ref[...] = o.T.astype(o_ref.dtype)

    @pl.when(qi == 0)
    def _():
        finish([(qi, 0)])

    @pl.when((qi >= 1) & (n_far % 2 == 0))
    def _():
        finish([(qi - 1, 1), (qi, 0)])

    @pl.when(n_far % 2 == 1)
    def _():
        finish([(n_far - 1, None), (qi - 1, 1), (qi, 0)])


def diff_attention_core(qt, k, vt, bias, lam_params, subln_w, lambda_init, tq, cw=512):
    t, width = k.shape
    hd = 2 * ATTN_HEAD_DIM
    heads = width // hd
    vrows = hd + ONES_ROWS
    assert t % tq == 0 and tq >= MAX_DISTANCE and vt.shape[0] == heads * vrows
    return pl.pallas_call(
        functools.partial(_attn_kernel, tq=tq, cw=min(cw, tq), lambda_init=lambda_init),
        name="diff_attn",
        grid=(heads, t // tq),
        in_specs=[
            pl.BlockSpec((hd, tq), lambda h, i: (h, i)),
            pl.BlockSpec((t, hd), lambda h, i: (0, h)),
            pl.BlockSpec((vrows, t), lambda h, i: (h, 0)),
            pl.BlockSpec((1, 2, tq, tq), lambda h, i: (h, 0, 0, 0)),
            pl.BlockSpec((4, ATTN_HEAD_DIM), lambda h, i: (0, 0)),
            pl.BlockSpec((hd, 1), lambda h, i: (0, 0)),
        ],
        out_specs=pl.BlockSpec((tq, hd), lambda h, i: (i, h)),
        out_shape=jax.ShapeDtypeStruct((t, width), BF16),
        scratch_shapes=[pltpu.VMEM((1, 2 * tq), F32), pltpu.VMEM((1, 2 * tq), F32),
                        pltpu.VMEM((vrows, 2 * tq), F32), pltpu.VMEM((hd, 2 * tq), BF16),
                        pltpu.VMEM((tq, 2 * tq), F32)],
        compiler_params=_params("parallel", "arbitrary"),
    )(qt, k, vt, bias, lam_params.astype(F32), subln_w.reshape(hd, 1).astype(F32))


def _proj_residual_kernel(x_ref, a_ref, w_ref, o_ref):
    o_ref[...] = x_ref[...] + jnp.dot(a_ref[...], w_ref[...], preferred_element_type=F32)


def proj_residual(x, a, w, tm=512):
    t, d = x.shape
    k = a.shape[1]
    tm = min(tm, t)
    return pl.pallas_call(
        _proj_residual_kernel,
        name="proj_residual",
        grid=(t // tm,),
        in_specs=[
            pl.BlockSpec((tm, d), lambda i: (i, 0)),
            pl.BlockSpec((tm, k), lambda i: (i, 0)),
            pl.BlockSpec((k, d), lambda i: (0, 0)),
        ],
        out_specs=pl.BlockSpec((tm, d), lambda i: (i, 0)),
        out_shape=jax.ShapeDtypeStruct((t, d), F32),
        compiler_params=_params("parallel"),
    )(x, a, w)


def _swiglu_step(hn, wg, wu, wd):
    gate = jnp.dot(hn, wg, preferred_element_type=F32)
    up = jnp.dot(hn, wu, preferred_element_type=F32)
    act = gate * _sigmoid(gate) * up
    return jnp.dot(act.astype(BF16), wd, preferred_element_type=F32)


def _ffn_kernel(te_ref, first_ref, nu_ref, *refs, layer, normalize, tf):
    if normalize:
        x_ref, g_ref, wg_hbm, wu_hbm, wd_hbm, o_ref = refs[:6]
    else:
        x_ref, wg_hbm, wu_hbm, wd_hbm, o_ref = refs[:5]
    hn_ref, act_ref, wg_res, wu_res, wd_res, wg_stg, wu_stg, wd_stg, sem = refs[-9:]
    i = pl.program_id(0)
    nf = wg_res.shape[0]
    e = te_ref[i]

    def block_copies(jb, slot):
        f0 = jb * tf
        return (pltpu.make_async_copy(wg_hbm.at[layer, e, :, pl.ds(f0, tf)], wg_stg.at[slot], sem.at[0, slot]),
                pltpu.make_async_copy(wu_hbm.at[layer, e, :, pl.ds(f0, tf)], wu_stg.at[slot], sem.at[1, slot]),
                pltpu.make_async_copy(wd_hbm.at[layer, e, pl.ds(f0, tf), :], wd_stg.at[slot], sem.at[2, slot]))

    def tile(stream_weights):
        x = x_ref[...]
        if normalize:
            hn_ref[...] = _rms_rows(x, g_ref[...]).astype(BF16)
        else:
            hn_ref[...] = x.astype(BF16)
        hn = hn_ref[...]
        if stream_weights:
            for cp in block_copies(0, 0):
                cp.start()
        for jb in range(nf):
            if stream_weights:
                slot = jb % 2
                for cp in block_copies(jb, slot):
                    cp.wait()
                if jb + 1 < nf:
                    for cp in block_copies(jb + 1, 1 - slot):
                        cp.start()
                wg_res[jb] = wg_stg[slot].astype(BF16)
                wu_res[jb] = wu_stg[slot].astype(BF16)
                wd_res[jb * tf:(jb + 1) * tf, :] = wd_stg[slot].astype(BF16)
            gate = jnp.dot(hn, wg_res[jb], preferred_element_type=F32)
            up = jnp.dot(hn, wu_res[jb], preferred_element_type=F32)
            act_ref[:, jb * tf:(jb + 1) * tf] = (gate * _sigmoid(gate) * up).astype(BF16)
        y = jnp.dot(act_ref[...], wd_res[...], preferred_element_type=F32)
        o_ref[...] = x + y if normalize else y

    @pl.when(i < nu_ref[0])
    def _():
        @pl.when(first_ref[i] == 1)
        def _():
            tile(True)

        @pl.when(first_ref[i] != 1)
        def _():
            tile(False)

    @pl.when(i >= nu_ref[0])
    def _():
        o_ref[...] = jnp.zeros(o_ref.shape, F32)


def ffn_tiles(x, tile_expert, first, n_used, w_gate, w_up, w_down, layer, gain=None, tm=512, tf=512):
    rows, d = x.shape
    f = w_gate.shape[3]
    nf = f // tf
    assert rows % tm == 0 and f % tf == 0
    normalize = gain is not None

    def row_tile(i, te, fr, nu):
        return (jnp.maximum(jnp.minimum(i, nu[0] - 1), 0), 0)

    in_specs = [pl.BlockSpec((tm, d), row_tile)]
    args = [x]
    if normalize:
        in_specs.append(pl.BlockSpec((1, d), lambda i, te, fr, nu: (0, 0)))
        args.append(gain.reshape(1, d))
    in_specs += [pl.BlockSpec(memory_space=pl.ANY)] * 3
    return pl.pallas_call(
        functools.partial(_ffn_kernel, layer=layer, normalize=normalize, tf=tf),
        name="ffn",
        grid_spec=pltpu.PrefetchScalarGridSpec(
            num_scalar_prefetch=3,
            grid=(rows // tm,),
            in_specs=in_specs,
            out_specs=pl.BlockSpec((tm, d), lambda i, te, fr, nu: (i, 0)),
            scratch_shapes=[pltpu.VMEM((tm, d), BF16), pltpu.VMEM((tm, f), BF16),
                            pltpu.VMEM((nf, d, tf), BF16), pltpu.VMEM((nf, d, tf), BF16),
                            pltpu.VMEM((f, d), BF16),
                            pltpu.VMEM((2, d, tf), F32), pltpu.VMEM((2, d, tf), F32),
                            pltpu.VMEM((2, tf, d), F32),
                            pltpu.SemaphoreType.DMA((3, 2))],
        ),
        out_shape=jax.ShapeDtypeStruct((rows, d), F32),
        compiler_params=_params("arbitrary"),
    )(tile_expert, first, n_used, *args, w_gate, w_up, w_down)


def ffn_residual(x, gain, w_gate, w_up, w_down, layer, tm=512):
    n_tiles = x.shape[0] // min(tm, x.shape[0])
    tiles = jnp.arange(n_tiles, dtype=I32)
    return ffn_tiles(x, jnp.zeros((n_tiles,), I32), (tiles == 0).astype(I32),
                     jnp.full((1,), n_tiles, I32), w_gate[:, None], w_up[:, None], w_down[:, None],
                     layer, gain=gain, tm=min(tm, x.shape[0]))


R_E1, R_E2, R_RANK1, R_RANK2, R_W1, R_W2 = range(6)


def _router_kernel(x_ref, g_ref, r_ref, tri_ref, route_ref, cnt_ref, base_ref, *, n_experts):
    @pl.when(pl.program_id(0) == 0)
    def _():
        base_ref[...] = jnp.zeros(base_ref.shape, F32)

    hn = _rms_rows(x_ref[...], g_ref[...])
    logits = jnp.dot(hn, r_ref[...], preferred_element_type=F32, precision=lax.Precision.HIGHEST)
    lane = lax.broadcasted_iota(I32, logits.shape, 1)
    lg = jnp.where(lane < n_experts, logits, MASK_VALUE)
    m1 = jnp.max(lg, axis=-1, keepdims=True)
    i1 = jnp.min(jnp.where(lg == m1, lane, LANES), axis=-1, keepdims=True)
    lg2 = jnp.where(lane == i1, MASK_VALUE, lg)
    m2 = jnp.max(lg2, axis=-1, keepdims=True)
    i2 = jnp.min(jnp.where(lg2 == m2, lane, LANES), axis=-1, keepdims=True)
    e2 = jnp.exp(m2 - m1)
    w1 = 1.0 / (1.0 + e2)
    w2 = e2 / (1.0 + e2)

    sel1 = lane == i1
    sel2 = lane == i2
    mask = jnp.where(sel1 | sel2, 1.0, 0.0)
    rank = base_ref[...] + jnp.dot(tri_ref[...], mask.astype(BF16), preferred_element_type=F32)
    rank1 = jnp.sum(jnp.where(sel1, rank, 0.0), axis=-1, keepdims=True)
    rank2 = jnp.sum(jnp.where(sel2, rank, 0.0), axis=-1, keepdims=True)
    base_ref[...] += jnp.sum(mask, axis=0, keepdims=True)

    rec = jnp.zeros(logits.shape, F32)
    for slot, val in ((R_E1, i1.astype(F32)), (R_E2, i2.astype(F32)), (R_RANK1, rank1),
                      (R_RANK2, rank2), (R_W1, w1), (R_W2, w2)):
        rec = jnp.where(lane == slot, val, rec)
    route_ref[...] = rec
    cnt_ref[...] = jnp.broadcast_to(base_ref[...], cnt_ref.shape)


def router(x, gain, w_router, tm=512):
    t, d = x.shape
    ne = w_router.shape[1]
    tm = min(tm, t)
    r_pad = jnp.zeros((d, LANES), F32).at[:, :ne].set(w_router.astype(F32))
    idx = jnp.arange(tm)
    tri = (idx[None, :] < idx[:, None]).astype(BF16)
    return pl.pallas_call(
        functools.partial(_router_kernel, n_experts=ne),
        name="router",
        grid=(t // tm,),
        in_specs=[pl.BlockSpec((tm, d), lambda i: (i, 0)),
                  pl.BlockSpec((1, d), lambda i: (0, 0)),
                  pl.BlockSpec((d, LANES), lambda i: (0, 0)),
                  pl.BlockSpec((tm, tm), lambda i: (0, 0))],
        out_specs=[pl.BlockSpec((tm, LANES), lambda i: (i, 0)),
                   pl.BlockSpec((8, LANES), lambda i: (0, 0))],
        out_shape=[jax.ShapeDtypeStruct((t, LANES), F32),
                   jax.ShapeDtypeStruct((8, LANES), F32)],
        scratch_shapes=[pltpu.VMEM((1, LANES), F32)],
        compiler_params=_params("arbitrary"),
    )(x, gain.reshape(1, d), r_pad, tri)


def _row_copy(src, r, dst, p, sem):
    return pltpu.make_async_copy(src.at[pl.ds(r, 1), :], dst.at[pl.ds(p, 1), :], sem)


def _dispatch_kernel(pos_ref, x_ref, g_ref, xs_in_ref, xs_ref, hn_ref, sem, *, tm):
    del xs_in_ref
    base = pl.program_id(0) * tm
    hn_ref[...] = _rms_rows(x_ref[...], g_ref[...])

    def issue(r, carry):
        for k in range(TOP_K):
            _row_copy(hn_ref, r, xs_ref, pos_ref[TOP_K * (base + r) + k], sem).start()
        return carry

    lax.fori_loop(0, tm, issue, 0, unroll=8)

    def drain(r, carry):
        for k in range(TOP_K):
            _row_copy(hn_ref, r, xs_ref, pos_ref[TOP_K * (base + r) + k], sem).wait()
        return carry

    lax.fori_loop(0, tm, drain, 0, unroll=8)


def dispatch(x, gain, pos, n_rows, tm=256):
    t, d = x.shape
    tm = min(tm, t)
    zeros = jnp.zeros((n_rows, d), F32)
    return pl.pallas_call(
        functools.partial(_dispatch_kernel, tm=tm),
        name="dispatch",
        grid_spec=pltpu.PrefetchScalarGridSpec(
            num_scalar_prefetch=1,
            grid=(t // tm,),
            in_specs=[pl.BlockSpec((tm, d), lambda i, pos: (i, 0)),
                      pl.BlockSpec((1, d), lambda i, pos: (0, 0)),
                      pl.BlockSpec(memory_space=pl.ANY)],
            out_specs=pl.BlockSpec(memory_space=pl.ANY),
            scratch_shapes=[pltpu.VMEM((tm, d), F32), pltpu.SemaphoreType.DMA(())],
        ),
        out_shape=jax.ShapeDtypeStruct((n_rows, d), F32),
        input_output_aliases={3: 0},
        compiler_params=_params("arbitrary"),
    )(pos, x, gain.reshape(1, d), zeros)


def _combine_kernel(pos_ref, x_ref, route_ref, ys_ref, o_ref, buf_ref, sem, *, tm):
    base = pl.program_id(0) * tm

    def issue(r, carry):
        for k in range(TOP_K):
            _row_copy(ys_ref, pos_ref[TOP_K * (base + r) + k], buf_ref.at[k], r, sem).start()
        return carry

    lax.fori_loop(0, tm, issue, 0, unroll=8)

    def drain(r, carry):
        for k in range(TOP_K):
            _row_copy(ys_ref, pos_ref[TOP_K * (base + r) + k], buf_ref.at[k], r, sem).wait()
        return carry

    lax.fori_loop(0, tm, drain, 0, unroll=8)

    rec = route_ref[...]
    lane = lax.broadcasted_iota(I32, rec.shape, 1)
    w1 = jnp.sum(jnp.where(lane == R_W1, rec, 0.0), axis=-1, keepdims=True)
    w2 = jnp.sum(jnp.where(lane == R_W2, rec, 0.0), axis=-1, keepdims=True)
    o_ref[...] = x_ref[...] + w1 * buf_ref[0] + w2 * buf_ref[1]


def combine(x, route, ys, pos, tm=256):
    t, d = x.shape
    tm = min(tm, t)
    return pl.pallas_call(
        functools.partial(_combine_kernel, tm=tm),
        name="combine",
        grid_spec=pltpu.PrefetchScalarGridSpec(
            num_scalar_prefetch=1,
            grid=(t // tm,),
            in_specs=[pl.BlockSpec((tm, d), lambda i, pos: (i, 0)),
                      pl.BlockSpec((tm, LANES), lambda i, pos: (i, 0)),
                      pl.BlockSpec(memory_space=pl.ANY)],
            out_specs=pl.BlockSpec((tm, d), lambda i, pos: (i, 0)),
            scratch_shapes=[pltpu.VMEM((TOP_K, tm, d), F32), pltpu.SemaphoreType.DMA(())],
        ),
        out_shape=jax.ShapeDtypeStruct((t, d), F32),
        compiler_params=_params("arbitrary"),
    )(pos, x, route, ys)


def _ssd_kernel(xbc_ref, z_ref, dt_ref, cw_ref, cb_ref, dtb_ref, alog_ref, dsk_ref, nw_ref,
                o_ref, prev_ref, state_ref, y_ref, *, d_inner):
    L = SSD_CHUNK
    gn = SSD_GROUPS * SSD_D_STATE
    n_pairs = d_inner // LANES
    pairs_per_group = n_pairs // SSD_GROUPS

    @pl.when(pl.program_id(0) == 0)
    def _():
        prev_ref[...] = jnp.zeros(prev_ref.shape, F32)
        state_ref[...] = jnp.zeros(state_ref.shape, F32)

    x = xbc_ref[...]
    xp = prev_ref[...]
    row1 = lax.broadcasted_iota(I32, (SUBLANES, 1), 0)
    cw = cw_ref[...]
    acc = cb_ref[...] + cw[SSD_CONV - 1:SSD_CONV] * x
    for k in range(1, SSD_CONV):
        shifted = pltpu.roll(x, k, 0)
        head = jnp.where(row1 < k, pltpu.roll(xp, k, 0), shifted[:SUBLANES])
        shifted = jnp.concatenate([head, shifted[SUBLANES:]], axis=0)
        acc = acc + cw[SSD_CONV - 1 - k:SSD_CONV - k] * shifted
    prev_ref[...] = x[L - SUBLANES:]
    xc = acc * _sigmoid(acc)

    dtr = dt_ref[...] + dtb_ref[...]
    dtv = jnp.maximum(dtr, 0.0) + jnp.log1p(jnp.exp(-jnp.abs(dtr)))
    da = dtv * (-jnp.exp(alog_ref[...]))
    rowh = lax.broadcasted_iota(I32, da.shape, 0)
    acs = da
    s = 1
    while s < L:
        acs = acs + jnp.where(rowh >= s, pltpu.roll(acs, s, 0), 0.0)
        s *= 2
    acs_t = acs.T
    tot = acs[L - 1:L, :]
    decay_to_end = jnp.exp(tot - acs)
    decay_from_start = jnp.exp(acs)
    tot_col = jnp.exp(acs_t[:, L - 1:L])

    lane = lax.broadcasted_iota(I32, (L, LANES), 1)
    rowl = lax.broadcasted_iota(I32, (L, LANES), 0)
    left = lane < SSD_HEAD_DIM
    causal = rowl >= lane

    def per_head(a, h0):
        return jnp.where(left, a[:, h0:h0 + 1], a[:, h0 + 1:h0 + 2])

    for g in range(SSD_GROUPS):
        bg = xc[:, d_inner + g * SSD_D_STATE:d_inner + (g + 1) * SSD_D_STATE].astype(BF16)
        cg = xc[:, d_inner + gn + g * SSD_D_STATE:d_inner + gn + (g + 1) * SSD_D_STATE].astype(BF16)
        cbm = lax.dot_general(cg, bg, (((1,), (1,)), ((), ())), preferred_element_type=F32)
        for jp in range(pairs_per_group):
            pj = g * pairs_per_group + jp
            h0 = 2 * pj
            xs_pair = xc[:, pj * LANES:(pj + 1) * LANES]
            xd = xs_pair * per_head(dtv, h0)
            y = jnp.zeros((L, LANES), F32)
            for half, hh in enumerate((h0, h0 + 1)):
                diff = acs[:, hh:hh + 1] - acs_t[hh:hh + 1, :]
                dec = jnp.exp(jnp.where(causal, diff, MASK_VALUE))
                in_half = left if half == 0 else jnp.logical_not(left)
                y = y + jnp.dot((cbm * dec).astype(BF16),
                                jnp.where(in_half, xd, 0.0).astype(BF16),
                                preferred_element_type=F32)
            s_old = state_ref[pj]
            y_off = lax.dot_general(cg, s_old.astype(BF16), (((1,), (1,)), ((), ())),
                                    preferred_element_type=F32)
            y = y + y_off * per_head(decay_from_start, h0)
            s_new = lax.dot_general((xd * per_head(decay_to_end, h0)).astype(BF16), bg,
                                    (((0,), (0,)), ((), ())), preferred_element_type=F32)
            carry = jnp.where(rowl < SSD_HEAD_DIM, tot_col[h0:h0 + 1, :], tot_col[h0 + 1:h0 + 2, :])
            state_ref[pj] = carry * s_old + s_new
            y_ref[:, pj * LANES:(pj + 1) * LANES] = y + dsk_ref[:, pj * LANES:(pj + 1) * LANES] * xs_pair

    z = z_ref[...].astype(F32)
    yg = y_ref[...] * (z * _sigmoid(z))
    gw = d_inner // SSD_GROUPS
    for g in range(SSD_GROUPS):
        blk = yg[:, g * gw:(g + 1) * gw]
        o_ref[:, g * gw:(g + 1) * gw] = _rms_rows(blk, nw_ref[:, g * gw:(g + 1) * gw]).astype(o_ref.dtype)


def ssd_core(xbc, z, dt, conv_w, conv_b, dt_bias, a_log, d_skip, norm_w):
    t = xbc.shape[0]
    d_inner = z.shape[1]
    conv_dim = xbc.shape[1]
    heads = d_inner // SSD_HEAD_DIM
    L = SSD_CHUNK
    assert t % L == 0 and heads <= LANES

    def lanes(v):
        return jnp.zeros((1, LANES), F32).at[0, :heads].set(v.astype(F32))

    row = lambda i: (i, 0)
    fixed = lambda i: (0, 0)
    return pl.pallas_call(
        functools.partial(_ssd_kernel, d_inner=d_inner),
        name="ssd",
        grid=(t // L,),
        in_specs=[
            pl.BlockSpec((L, conv_dim), row),
            pl.BlockSpec((L, d_inner), row),
            pl.BlockSpec((L, LANES), row),
            pl.BlockSpec((SSD_CONV, conv_dim), fixed),
            pl.BlockSpec((1, conv_dim), fixed),
            pl.BlockSpec((1, LANES), fixed),
            pl.BlockSpec((1, LANES), fixed),
            pl.BlockSpec((1, d_inner), fixed),
            pl.BlockSpec((1, d_inner), fixed),
        ],
        out_specs=pl.BlockSpec((L, d_inner), row),
        out_shape=jax.ShapeDtypeStruct((t, d_inner), BF16),
        scratch_shapes=[pltpu.VMEM((SUBLANES, conv_dim), F32),
                        pltpu.VMEM((d_inner // LANES, LANES, SSD_D_STATE), F32),
                        pltpu.VMEM((L, d_inner), F32)],
        compiler_params=_params("arbitrary"),
    )(xbc, z, dt, conv_w.astype(F32), conv_b.reshape(1, conv_dim).astype(F32),
      lanes(dt_bias), lanes(a_log),
      jnp.repeat(d_skip.astype(F32), SSD_HEAD_DIM).reshape(1, d_inner),
      norm_w.reshape(1, d_inner).astype(F32))


def lambda_init_for(layer_idx):
    return 0.8 - 0.6 * math.exp(-0.3 * layer_idx)


def _ssd_inproj_kernel(x_ref, g_ref, w_hbm, wdt_ref, z_ref, xbc_ref, dt_ref,
                       hn_ref, wres_ref, stg_ref, sem, *, layer, tn, d_inner):
    nb = wres_ref.shape[0]

    def block_copy(jb, slot):
        c0 = pl.multiple_of(jb * tn, tn)
        return pltpu.make_async_copy(w_hbm.at[layer, :, pl.ds(c0, tn)], stg_ref.at[slot], sem.at[slot])

    @pl.when(pl.program_id(0) == 0)
    def _():
        block_copy(0, 0).start()

        def body(jb, carry):
            slot = jb % 2
            block_copy(jb, slot).wait()

            @pl.when(jb + 1 < nb)
            def _():
                block_copy(jb + 1, 1 - slot).start()

            wres_ref[jb] = stg_ref[slot].astype(BF16)
            return carry

        lax.fori_loop(0, nb, body, 0)

    hn_ref[...] = _rms_rows(x_ref[...], g_ref[...]).astype(BF16)
    hn = hn_ref[...]
    for jb in range(nb):
        y = jnp.dot(hn, wres_ref[jb], preferred_element_type=F32)
        c0 = jb * tn
        if c0 < d_inner:
            z_ref[:, c0:c0 + tn] = y.astype(z_ref.dtype)
        else:
            xbc_ref[:, c0 - d_inner:c0 - d_inner + tn] = y
    dt_ref[...] = jnp.dot(hn, wdt_ref[...], preferred_element_type=F32)


def ssd_inproj(x, gain, w_in_all, layer, d_inner, conv_dim, heads, tm=512, tn=512):
    t, d = x.shape
    tm = min(tm, t)
    n_main = d_inner + conv_dim
    assert t % tm == 0 and d_inner % tn == 0 and conv_dim % tn == 0 and heads <= LANES
    w_dt = jnp.zeros((d, LANES), BF16).at[:, :heads].set(w_in_all[layer, :, n_main:].astype(BF16))
    row = lambda i: (i, 0)
    fixed = lambda i: (0, 0)
    return pl.pallas_call(
        functools.partial(_ssd_inproj_kernel, layer=layer, tn=tn, d_inner=d_inner),
        name="ssd_inproj",
        grid=(t // tm,),
        in_specs=[pl.BlockSpec((tm, d), row),
                  pl.BlockSpec((1, d), fixed),
                  pl.BlockSpec(memory_space=pl.ANY),
                  pl.BlockSpec((d, LANES), fixed)],
        out_specs=[pl.BlockSpec((tm, d_inner), row),
                   pl.BlockSpec((tm, conv_dim), row),
                   pl.BlockSpec((tm, LANES), row)],
        out_shape=[jax.ShapeDtypeStruct((t, d_inner), BF16),
                   jax.ShapeDtypeStruct((t, conv_dim), F32),
                   jax.ShapeDtypeStruct((t, LANES), F32)],
        scratch_shapes=[pltpu.VMEM((tm, d), BF16),
                        pltpu.VMEM((n_main // tn, d, tn), BF16),
                        pltpu.VMEM((2, d, tn), F32),
                        pltpu.SemaphoreType.DMA((2,))],
        compiler_params=_params("arbitrary"),
    )(x, gain.reshape(1, d), w_in_all, w_dt)


def ssd_layer(xt, gain, w_in_all, layer, conv_w, conv_b, dt_bias, a_log, d_skip, norm_w, w_out):
    d_inner = w_out.shape[0]
    conv_dim = conv_w.shape[1]
    heads = dt_bias.shape[0]
    z, xbc, dt = ssd_inproj(xt, gain, w_in_all, layer, d_inner, conv_dim, heads)
    y = ssd_core(xbc, z, dt, conv_w, conv_b, dt_bias, a_log, d_skip, norm_w)
    return proj_residual(xt, y, w_out.astype(BF16))


def moe_layer(xt, gain, w_router, w_gate, w_up, w_down, layer, tmg=512):
    t, d = xt.shape
    ne = w_router.shape[1]
    tmg = min(tmg, t)
    route, counts = router(xt, gain, w_router)
    counts = counts[0, :ne].astype(I32)
    padded = (counts + tmg - 1) // tmg * tmg
    ends = jnp.cumsum(padded)
    starts = ends - padded
    experts = route[:, R_E1:R_E2 + 1].astype(I32)
    ranks = route[:, R_RANK1:R_RANK2 + 1].astype(I32)
    pos = (jnp.take(starts, experts) + ranks).reshape(-1)
    n_rows = TOP_K * t + ne * tmg
    n_tiles = n_rows // tmg
    n_used = (ends[-1] // tmg).astype(I32)
    tile_row = jnp.minimum(jnp.arange(n_tiles, dtype=I32), n_used - 1) * tmg
    tile_expert = jnp.sum(tile_row[:, None] >= ends[None, :], axis=1).astype(I32)
    first = jnp.concatenate([jnp.ones((1,), I32), (tile_expert[1:] != tile_expert[:-1]).astype(I32)])
    xs = dispatch(xt, gain, pos, n_rows)
    ys = ffn_tiles(xs, tile_expert, first, n_used.reshape(1), w_gate, w_up, w_down, layer, tm=tmg)
    return combine(xt, route, ys, pos)


def kernel(x, rel_bias, norm_mix, norm_ffn, attn_w_in, attn_q_gain, attn_k_gain, attn_lambda, attn_subln, attn_w_out, ffn_w_gate, ffn_w_up, ffn_w_down, ssd_w_in, ssd_conv_w, ssd_conv_b, ssd_dt_bias, ssd_a_log, ssd_d, ssd_norm, ssd_w_out, moe_router, moe_w_gate, moe_w_up, moe_w_down):
    b, s, d = x.shape
    t = b * s
    assert b == 1
    depth = norm_mix.shape[0]
    xt = x.reshape(t, d).astype(F32)
    tq = min(512, t)
    bias = bias_tiles(rel_bias, tq)
    for i in range(depth):
        j = i // 2
        if i % 2 == 0:
            qt, k, vt = attn_inproj(xt, norm_mix[i], attn_w_in[j].astype(BF16), attn_q_gain[j], attn_k_gain[j])
            o = diff_attention_core(qt, k, vt, bias, attn_lambda[j], attn_subln[j], lambda_init_for(i), tq)
            xt = proj_residual(xt, o, attn_w_out[j].astype(BF16))
            xt = ffn_residual(xt, norm_ffn[i], ffn_w_gate, ffn_w_up, ffn_w_down, j)
        else:
            xt = ssd_layer(xt, norm_mix[i], ssd_w_in, j, ssd_conv_w[j], ssd_conv_b[j], ssd_dt_bias[j],
                           ssd_a_log[j], ssd_d[j], ssd_norm[j], ssd_w_out[j])
            xt = moe_layer(xt, norm_ffn[i], moe_router[j], moe_w_gate, moe_w_up, moe_w_down, j)
    return xt.reshape(b, s, d)
```

```python
import functools
import math

import jax
import jax.numpy as jnp
from jax import lax
from jax.experimental import pallas as pl
from jax.experimental.pallas import tpu as pltpu

F32 = jnp.float32
BF16 = jnp.bfloat16
I32 = jnp.int32

NORM_EPS = 1e-6
LANES = 128
SUBLANES = 8
MASK_VALUE = -1e30
LOG2E = math.log2(math.e)
ONES_ROWS = 16

ATTN_HEAD_DIM = 64
NUM_BUCKETS = 32
MAX_EXACT = NUM_BUCKETS // 2
MAX_DISTANCE = 128
SSD_HEAD_DIM = 64
SSD_GROUPS = 4
SSD_D_STATE = 128
SSD_CONV = 4
SSD_CHUNK = 128
TOP_K = 2

VMEM_LIMIT = 56 * 1024 * 1024


def _params(*sem):
    return pltpu.CompilerParams(dimension_semantics=sem, vmem_limit_bytes=VMEM_LIMIT)


def _sigmoid(x):
    return 1.0 / (1.0 + jnp.exp(-x))


def _rms_rows(x, gain):
    ms = jnp.mean(x * x, axis=-1, keepdims=True)
    return x * lax.rsqrt(ms + NORM_EPS) * gain


def _attn_inproj_kernel(x_ref, g_ref, w_ref, qkg_ref, bd_ref, qt_ref, k_ref, vt_ref, hn_ref):
    hn_ref[...] = _rms_rows(x_ref[...], g_ref[...]).astype(BF16)
    hn = hn_ref[...]
    width = k_ref.shape[1]
    bd = bd_ref[...]
    w2 = bd.shape[0]

    def head_norm(y, gain_row):
        y2 = (y * y).astype(BF16)
        ms = jnp.concatenate(
            [jnp.dot(y2[:, c * w2:(c + 1) * w2], bd, preferred_element_type=F32)
             for c in range(width // w2)], axis=1)
        return y * lax.rsqrt(ms + NORM_EPS) * gain_row

    q = jnp.dot(hn, w_ref[:, 0:width], preferred_element_type=F32)
    qt_ref[...] = head_norm(q, qkg_ref[0:1, :]).T.astype(BF16)
    k = jnp.dot(hn, w_ref[:, width:2 * width], preferred_element_type=F32)
    k_ref[...] = head_norm(k, qkg_ref[1:2, :]).astype(BF16)
    v = jnp.dot(hn, w_ref[:, 2 * width:3 * width], preferred_element_type=F32)
    vt = v.astype(BF16).T
    hd = 2 * ATTN_HEAD_DIM
    ones = jnp.ones((ONES_ROWS, vt.shape[1]), BF16)
    for h in range(width // hd):
        r0 = h * (hd + ONES_ROWS)
        vt_ref[r0:r0 + hd, :] = vt[h * hd:(h + 1) * hd, :]
        vt_ref[r0 + hd:r0 + hd + ONES_ROWS, :] = ones


def attn_inproj(x, gain, w_in, q_gain, k_gain, tm=512):
    t, d = x.shape
    width = w_in.shape[1] // 3
    tm = min(tm, t)
    reps = width // ATTN_HEAD_DIM
    qk_gain = jnp.stack([jnp.tile(q_gain, reps) * (ATTN_HEAD_DIM ** -0.5 * LOG2E),
                         jnp.tile(k_gain, reps)]).astype(F32)
    hd = 2 * ATTN_HEAD_DIM
    vt_rows = (width // hd) * (hd + ONES_ROWS)
    bw = 256
    grp = jnp.arange(bw) // ATTN_HEAD_DIM
    bd = jnp.where(grp[:, None] == grp[None, :], 1.0 / ATTN_HEAD_DIM, 0.0).astype(BF16)
    return pl.pallas_call(
        _attn_inproj_kernel,
        name="attn_inproj",
        grid=(t // tm,),
        in_specs=[
            pl.BlockSpec((tm, d), lambda i: (i, 0)),
            pl.BlockSpec((1, d), lambda i: (0, 0)),
            pl.BlockSpec((d, 3 * width), lambda i: (0, 0)),
            pl.BlockSpec((2, width), lambda i: (0, 0)),
            pl.BlockSpec((bw, bw), lambda i: (0, 0)),
        ],
        out_specs=[pl.BlockSpec((width, tm), lambda i: (0, i)),
                   pl.BlockSpec((tm, width), lambda i: (i, 0)),
                   pl.BlockSpec((vt_rows, tm), lambda i: (0, i))],
        out_shape=[jax.ShapeDtypeStruct((width, t), BF16),
                   jax.ShapeDtypeStruct((t, width), BF16),
                   jax.ShapeDtypeStruct((vt_rows, t), BF16)],
        scratch_shapes=[pltpu.VMEM((tm, d), BF16)],
        compiler_params=_params("parallel"),
    )(x, gain.reshape(1, d), w_in, qk_gain, bd)


def _bias_tiles_kernel(tab_ref, o_ref, *, tq):
    h = pl.program_id(0)
    r = lax.broadcasted_iota(I32, (tq, tq), 0)
    c = lax.broadcasted_iota(I32, (tq, tq), 1)
    far = tab_ref[NUM_BUCKETS - 1, h]
    for d in range(2):
        dist = c - r + d * tq
        n = jnp.maximum(dist, 0)
        nf = jnp.maximum(n, 1).astype(F32)
        large = MAX_EXACT + (jnp.log(nf / MAX_EXACT) / math.log(MAX_DISTANCE / MAX_EXACT)
                             * (NUM_BUCKETS - MAX_EXACT)).astype(I32)
        large = jnp.minimum(large, NUM_BUCKETS - 1)
        bucket = jnp.where(n < MAX_EXACT, n, large)
        bias = jnp.zeros((tq, tq), F32)
        for b in range(NUM_BUCKETS):
            bias = jnp.where(bucket == b, tab_ref[b, h], bias)
        o_ref[0, d] = jnp.where(dist >= 0, (bias - far) * LOG2E, MASK_VALUE)


def bias_tiles(rel_bias, tq):
    nb, h = rel_bias.shape
    return pl.pallas_call(
        functools.partial(_bias_tiles_kernel, tq=tq),
        name="bias_tiles",
        grid=(h,),
        in_specs=[pl.BlockSpec(memory_space=pltpu.SMEM)],
        out_specs=pl.BlockSpec((1, 2, tq, tq), lambda i: (i, 0, 0, 0)),
        out_shape=jax.ShapeDtypeStruct((h, 2, tq, tq), F32),
        compiler_params=_params("parallel"),
    )(rel_bias.astype(F32))


def _attn_kernel(*refs, tq, cw, lambda_init):
    n_q = refs[1].shape[0] // tq

    def q_body(qi, carry):
        _attn_q_block(qi, *refs, tq=tq, cw=cw, lambda_init=lambda_init)
        return carry

    lax.fori_loop(0, n_q, q_body, 0)


def _attn_q_block(qi, qt_ref, k_ref, vt_ref, bias_ref, lam_ref, sub_ref, o_ref,
                  m_ref, smax_ref, acc_ref, qbd_ref, s_ref, *, tq, cw, lambda_init):
    d = ATTN_HEAD_DIM
    hd = 2 * d
    q0 = pl.multiple_of(qi * tq, tq)
    qt = qt_ref[:, pl.ds(q0, tq)]
    row = lax.broadcasted_iota(I32, qt.shape, 0)
    zero = jnp.zeros_like(qt)
    qbd_ref[:, :tq] = jnp.where(row < d, qt, zero)
    qbd_ref[:, tq:] = jnp.where(row >= d, qt, zero)

    m_ref[...] = jnp.full(m_ref.shape, MASK_VALUE, F32)
    acc_ref[...] = jnp.zeros(acc_ref.shape, F32)

    n_chunks = 2 * tq // cw

    def block_start(j):
        return pl.multiple_of(j * tq, tq)

    def scores(j, c):
        kb = k_ref[pl.ds(block_start(j), tq), :]
        s = jnp.dot(kb, qbd_ref[:, c * cw:(c + 1) * cw], preferred_element_type=F32)
        s_ref[:, c * cw:(c + 1) * cw] = s
        smax_ref[:, c * cw:(c + 1) * cw] = jnp.max(s, axis=0, keepdims=True)

    def stage(j, bias_idx, has_next):
        vtb = vt_ref[:, pl.ds(block_start(j), tq)]
        for c in range(n_chunks):
            cols = slice(c * cw, (c + 1) * cw)
            s = s_ref[:, cols]
            if bias_idx is None:
                s_max = smax_ref[:, cols]
            else:
                b0 = (c * cw) % tq
                s = s + bias_ref[0, bias_idx, :, b0:b0 + cw]
                s_max = jnp.max(s, axis=0, keepdims=True)
            m_prev = m_ref[:, cols]
            m_new = jnp.maximum(m_prev, s_max)
            p = jnp.exp2(s - m_new)
            if has_next:
                scores(j + 1, c)
            alpha = jnp.exp2(m_prev - m_new)
            acc_ref[:, cols] = alpha * acc_ref[:, cols] + jnp.dot(vtb, p.astype(BF16),
                                                                  preferred_element_type=F32)
            m_ref[:, cols] = m_new

    for c in range(n_chunks):
        scores(0, c)

    n_far = jnp.maximum(qi - 1, 0)

    def far_quad(i, carry):
        for u in range(4):
            stage(4 * i + u, None, True)
        return carry

    lax.fori_loop(0, n_far // 4, far_quad, 0)

    @pl.when(n_far % 4 >= 2)
    def _():
        first = n_far // 4 * 4
        stage(first, None, True)
        stage(first + 1, None, True)

    def finish(blocks):
        for idx, (j, bias_idx) in enumerate(blocks):
            stage(j, bias_idx, idx + 1 < len(blocks))
        o = acc_ref[0:hd, :] / acc_ref[hd:hd + 1, :]
        lp = lam_ref[...]
        lam = (jnp.exp(jnp.sum(lp[0:1] * lp[1:2], axis=-1, keepdims=True))
               - jnp.exp(jnp.sum(lp[2:3] * lp[3:4], axis=-1, keepdims=True)) + lambda_init)
        o = o[:, :tq] - lam * o[:, tq:]
        ms = jnp.mean(o * o, axis=0, keepdims=True)
        o = o * lax.rsqrt(ms + NORM_EPS) * sub_ref[...] * (1.0 - lambda_init)
        o_ref[pl.ds(q0, tq), :] = o.T.astype(o_ref.dtype)

    @pl.when(qi == 0)
    def _():
        finish([(qi, 0)])

    @pl.when((qi >= 1) & (n_far % 2 == 0))
    def _():
        finish([(qi - 1, 1), (qi, 0)])

    @pl.when(n_far % 2 == 1)
    def _():
        finish([(n_far - 1, None), (qi - 1, 1), (qi, 0)])


def diff_attention_core(qt, k, vt, bias, lam_params, subln_w, lambda_init, tq, cw=512):
    t, width = k.shape
    hd = 2 * ATTN_HEAD_DIM
    heads = width // hd
    vrows = hd + ONES_ROWS
    assert t % tq == 0 and tq >= MAX_DISTANCE and vt.shape[0] == heads * vrows
    return pl.pallas_call(
        functools.partial(_attn_kernel, tq=tq, cw=min(cw, tq), lambda_init=lambda_init),
        name="diff_attn",
        grid=(heads,),
        in_specs=[
            pl.BlockSpec((hd, t), lambda h: (h, 0)),
            pl.BlockSpec((t, hd), lambda h: (0, h)),
            pl.BlockSpec((vrows, t), lambda h: (h, 0)),
            pl.BlockSpec((1, 2, tq, tq), lambda h: (h, 0, 0, 0)),
            pl.BlockSpec((4, ATTN_HEAD_DIM), lambda h: (0, 0)),
            pl.BlockSpec((hd, 1), lambda h: (0, 0)),
        ],
        out_specs=pl.BlockSpec((t, hd), lambda h: (0, h)),
        out_shape=jax.ShapeDtypeStruct((t, width), BF16),
        scratch_shapes=[pltpu.VMEM((1, 2 * tq), F32), pltpu.VMEM((1, 2 * tq), F32),
                        pltpu.VMEM((vrows, 2 * tq), F32), pltpu.VMEM((hd, 2 * tq), BF16),
                        pltpu.VMEM((tq, 2 * tq), F32)],
        compiler_params=_params("parallel"),
    )(qt, k, vt, bias, lam_params.astype(F32), subln_w.reshape(hd, 1).astype(F32))


def _proj_residual_kernel(x_ref, a_ref, w_ref, o_ref):
    o_ref[...] = x_ref[...] + jnp.dot(a_ref[...], w_ref[...], preferred_element_type=F32)


def proj_residual(x, a, w, tm=1024):
    t, d = x.shape
    k = a.shape[1]
    tm = min(tm, t)
    return pl.pallas_call(
        _proj_residual_kernel,
        name="proj_residual",
        grid=(t // tm,),
        in_specs=[
            pl.BlockSpec((tm, d), lambda i: (i, 0)),
            pl.BlockSpec((tm, k), lambda i: (i, 0)),
            pl.BlockSpec((k, d), lambda i: (0, 0)),
        ],
        out_specs=pl.BlockSpec((tm, d), lambda i: (i, 0)),
        out_shape=jax.ShapeDtypeStruct((t, d), F32),
        compiler_params=_params("parallel"),
    )(x, a, w)


def _swiglu_step(hn, wg, wu, wd):
    gate = jnp.dot(hn, wg, preferred_element_type=F32)
    up = jnp.dot(hn, wu, preferred_element_type=F32)
    act = gate * _sigmoid(gate) * up
    return jnp.dot(act.astype(BF16), wd, preferred_element_type=F32)


def _ffn_kernel(te_ref, first_ref, nu_ref, *refs, layer, normalize, tf):
    if normalize:
        x_ref, g_ref, wg_hbm, wu_hbm, wd_hbm, o_ref = refs[:6]
    else:
        x_ref, wg_hbm, wu_hbm, wd_hbm, o_ref = refs[:5]
    hn_ref, act_ref, wg_res, wu_res, wd_res, wg_stg, wu_stg, wd_stg, sem = refs[-9:]
    i = pl.program_id(0)
    nf = wg_res.shape[0]
    e = te_ref[i]

    def block_copies(jb, slot):
        f0 = jb * tf
        return (pltpu.make_async_copy(wg_hbm.at[layer, e, :, pl.ds(f0, tf)], wg_stg.at[slot], sem.at[0, slot]),
                pltpu.make_async_copy(wu_hbm.at[layer, e, :, pl.ds(f0, tf)], wu_stg.at[slot], sem.at[1, slot]),
                pltpu.make_async_copy(wd_hbm.at[layer, e, pl.ds(f0, tf), :], wd_stg.at[slot], sem.at[2, slot]))

    def tile(stream_weights):
        x = x_ref[...]
        if normalize:
            hn_ref[...] = _rms_rows(x, g_ref[...]).astype(BF16)
        else:
            hn_ref[...] = x.astype(BF16)
        hn = hn_ref[...]
        if stream_weights:
            for cp in block_copies(0, 0):
                cp.start()
        for jb in range(nf):
            if stream_weights:
                slot = jb % 2
                for cp in block_copies(jb, slot):
                    cp.wait()
                if jb + 1 < nf:
                    for cp in block_copies(jb + 1, 1 - slot):
                        cp.start()
                wg_res[jb] = wg_stg[slot].astype(BF16)
                wu_res[jb] = wu_stg[slot].astype(BF16)
                wd_res[jb * tf:(jb + 1) * tf, :] = wd_stg[slot].astype(BF16)
            gate = jnp.dot(hn, wg_res[jb], preferred_element_type=F32)
            up = jnp.dot(hn, wu_res[jb], preferred_element_type=F32)
            act_ref[:, jb * tf:(jb + 1) * tf] = (gate * _sigmoid(gate) * up).astype(BF16)
        y = jnp.dot(act_ref[...], wd_res[...], preferred_element_type=F32)
        o_ref[...] = x + y if normalize else y

    @pl.when(i < nu_ref[0])
    def _():
        @pl.when(first_ref[i] == 1)
        def _():
            tile(True)

        @pl.when(first_ref[i] != 1)
        def _():
            tile(False)

    @pl.when(i >= nu_ref[0])
    def _():
        o_ref[...] = jnp.zeros(o_ref.shape, F32)


def ffn_tiles(x, tile_expert, first, n_used, w_gate, w_up, w_down, layer, gain=None, tm=512, tf=512):
    rows, d = x.shape
    f = w_gate.shape[3]
    nf = f // tf
    assert rows % tm == 0 and f % tf == 0
    normalize = gain is not None

    def row_tile(i, te, fr, nu):
        return (jnp.maximum(jnp.minimum(i, nu[0] - 1), 0), 0)

    in_specs = [pl.BlockSpec((tm, d), row_tile)]
    args = [x]
    if normalize:
        in_specs.append(pl.BlockSpec((1, d), lambda i, te, fr, nu: (0, 0)))
        args.append(gain.reshape(1, d))
    in_specs += [pl.BlockSpec(memory_space=pl.ANY)] * 3
    return pl.pallas_call(
        functools.partial(_ffn_kernel, layer=layer, normalize=normalize, tf=tf),
        name="ffn",
        grid_spec=pltpu.PrefetchScalarGridSpec(
            num_scalar_prefetch=3,
            grid=(rows // tm,),
            in_specs=in_specs,
            out_specs=pl.BlockSpec((tm, d), lambda i, te, fr, nu: (i, 0)),
            scratch_shapes=[pltpu.VMEM((tm, d), BF16), pltpu.VMEM((tm, f), BF16),
                            pltpu.VMEM((nf, d, tf), BF16), pltpu.VMEM((nf, d, tf), BF16),
                            pltpu.VMEM((f, d), BF16),
                            pltpu.VMEM((2, d, tf), F32), pltpu.VMEM((2, d, tf), F32),
                            pltpu.VMEM((2, tf, d), F32),
                            pltpu.SemaphoreType.DMA((3, 2))],
        ),
        out_shape=jax.ShapeDtypeStruct((rows, d), F32),
        compiler_params=_params("arbitrary"),
    )(tile_expert, first, n_used, *args, w_gate, w_up, w_down)


def ffn_residual(x, gain, w_gate, w_up, w_down, layer, tm=512):
    n_tiles = x.shape[0] // min(tm, x.shape[0])
    tiles = jnp.arange(n_tiles, dtype=I32)
    return ffn_tiles(x, jnp.zeros((n_tiles,), I32), (tiles == 0).astype(I32),
                     jnp.full((1,), n_tiles, I32), w_gate[:, None], w_up[:, None], w_down[:, None],
                     layer, gain=gain, tm=min(tm, x.shape[0]))


R_E1, R_E2, R_RANK1, R_RANK2, R_W1, R_W2 = range(6)


def _router_kernel(x_ref, g_ref, r_ref, tri_ref, route_ref, cnt_ref, base_ref, *, n_experts):
    @pl.when(pl.program_id(0) == 0)
    def _():
        base_ref[...] = jnp.zeros(base_ref.shape, F32)

    hn = _rms_rows(x_ref[...], g_ref[...])
    logits = jnp.dot(hn, r_ref[...], preferred_element_type=F32, precision=lax.Precision.HIGHEST)
    lane = lax.broadcasted_iota(I32, logits.shape, 1)
    lg = jnp.where(lane < n_experts, logits, MASK_VALUE)
    m1 = jnp.max(lg, axis=-1, keepdims=True)
    i1 = jnp.min(jnp.where(lg == m1, lane, LANES), axis=-1, keepdims=True)
    lg2 = jnp.where(lane == i1, MASK_VALUE, lg)
    m2 = jnp.max(lg2, axis=-1, keepdims=True)
    i2 = jnp.min(jnp.where(lg2 == m2, lane, LANES), axis=-1, keepdims=True)
    e2 = jnp.exp(m2 - m1)
    w1 = 1.0 / (1.0 + e2)
    w2 = e2 / (1.0 + e2)

    sel1 = lane == i1
    sel2 = lane == i2
    mask = jnp.where(sel1 | sel2, 1.0, 0.0)
    rank = base_ref[...] + jnp.dot(tri_ref[...], mask.astype(BF16), preferred_element_type=F32)
    rank1 = jnp.sum(jnp.where(sel1, rank, 0.0), axis=-1, keepdims=True)
    rank2 = jnp.sum(jnp.where(sel2, rank, 0.0), axis=-1, keepdims=True)
    base_ref[...] += jnp.sum(mask, axis=0, keepdims=True)

    rec = jnp.zeros(logits.shape, F32)
    for slot, val in ((R_E1, i1.astype(F32)), (R_E2, i2.astype(F32)), (R_RANK1, rank1),
                      (R_RANK2, rank2), (R_W1, w1), (R_W2, w2)):
        rec = jnp.where(lane == slot, val, rec)
    route_ref[...] = rec
    cnt_ref[...] = jnp.broadcast_to(base_ref[...], cnt_ref.shape)


def router(x, gain, w_router, tm=1024):
    t, d = x.shape
    ne = w_router.shape[1]
    tm = min(tm, t)
    r_pad = jnp.zeros((d, LANES), F32).at[:, :ne].set(w_router.astype(F32))
    idx = jnp.arange(tm)
    tri = (idx[None, :] < idx[:, None]).astype(BF16)
    return pl.pallas_call(
        functools.partial(_router_kernel, n_experts=ne),
        name="router",
        grid=(t // tm,),
        in_specs=[pl.BlockSpec((tm, d), lambda i: (i, 0)),
                  pl.BlockSpec((1, d), lambda i: (0, 0)),
                  pl.BlockSpec((d, LANES), lambda i: (0, 0)),
                  pl.BlockSpec((tm, tm), lambda i: (0, 0))],
        out_specs=[pl.BlockSpec((tm, LANES), lambda i: (i, 0)),
                   pl.BlockSpec((8, LANES), lambda i: (0, 0))],
        out_shape=[jax.ShapeDtypeStruct((t, LANES), F32),
                   jax.ShapeDtypeStruct((8, LANES), F32)],
        scratch_shapes=[pltpu.VMEM((1, LANES), F32)],
        compiler_params=_params("arbitrary"),
    )(x, gain.reshape(1, d), r_pad, tri)


def _row_copy(src, r, dst, p, sem):
    return pltpu.make_async_copy(src.at[pl.ds(r, 1), :], dst.at[pl.ds(p, 1), :], sem)


def _dispatch_kernel(pos_ref, x_ref, g_ref, xs_in_ref, xs_ref, hn_ref, sem, *, tm):
    del xs_in_ref
    base = pl.program_id(0) * tm
    hn_ref[...] = _rms_rows(x_ref[...], g_ref[...])

    def issue(r, carry):
        for k in range(TOP_K):
            _row_copy(hn_ref, r, xs_ref, pos_ref[TOP_K * (base + r) + k], sem).start()
        return carry

    lax.fori_loop(0, tm, issue, 0, unroll=8)

    def drain(r, carry):
        for k in range(TOP_K):
            _row_copy(hn_ref, r, xs_ref, pos_ref[TOP_K * (base + r) + k], sem).wait()
        return carry

    lax.fori_loop(0, tm, drain, 0, unroll=8)


def dispatch(x, gain, pos, n_rows, tm=512):
    t, d = x.shape
    tm = min(tm, t)
    zeros = jnp.zeros((n_rows, d), F32)
    return pl.pallas_call(
        functools.partial(_dispatch_kernel, tm=tm),
        name="dispatch",
        grid_spec=pltpu.PrefetchScalarGridSpec(
            num_scalar_prefetch=1,
            grid=(t // tm,),
            in_specs=[pl.BlockSpec((tm, d), lambda i, pos: (i, 0)),
                      pl.BlockSpec((1, d), lambda i, pos: (0, 0)),
                      pl.BlockSpec(memory_space=pl.ANY)],
            out_specs=pl.BlockSpec(memory_space=pl.ANY),
            scratch_shapes=[pltpu.VMEM((tm, d), F32), pltpu.SemaphoreType.DMA(())],
        ),
        out_shape=jax.ShapeDtypeStruct((n_rows, d), F32),
        input_output_aliases={3: 0},
        compiler_params=_params("arbitrary"),
    )(pos, x, gain.reshape(1, d), zeros)


def _combine_kernel(pos_ref, x_ref, route_ref, ys_ref, o_ref, buf_ref, sem, *, tm):
    base = pl.program_id(0) * tm

    def issue(r, carry):
        for k in range(TOP_K):
            _row_copy(ys_ref, pos_ref[TOP_K * (base + r) + k], buf_ref.at[k], r, sem).start()
        return carry

    lax.fori_loop(0, tm, issue, 0, unroll=8)

    def drain(r, carry):
        for k in range(TOP_K):
            _row_copy(ys_ref, pos_ref[TOP_K * (base + r) + k], buf_ref.at[k], r, sem).wait()
        return carry

    lax.fori_loop(0, tm, drain, 0, unroll=8)

    rec = route_ref[...]
    lane = lax.broadcasted_iota(I32, rec.shape, 1)
    w1 = jnp.sum(jnp.where(lane == R_W1, rec, 0.0), axis=-1, keepdims=True)
    w2 = jnp.sum(jnp.where(lane == R_W2, rec, 0.0), axis=-1, keepdims=True)
    o_ref[...] = x_ref[...] + w1 * buf_ref[0] + w2 * buf_ref[1]


def combine(x, route, ys, pos, tm=512):
    t, d = x.shape
    tm = min(tm, t)
    return pl.pallas_call(
        functools.partial(_combine_kernel, tm=tm),
        name="combine",
        grid_spec=pltpu.PrefetchScalarGridSpec(
            num_scalar_prefetch=1,
            grid=(t // tm,),
            in_specs=[pl.BlockSpec((tm, d), lambda i, pos: (i, 0)),
                      pl.BlockSpec((tm, LANES), lambda i, pos: (i, 0)),
                      pl.BlockSpec(memory_space=pl.ANY)],
            out_specs=pl.BlockSpec((tm, d), lambda i, pos: (i, 0)),
            scratch_shapes=[pltpu.VMEM((TOP_K, tm, d), F32), pltpu.SemaphoreType.DMA(())],
        ),
        out_shape=jax.ShapeDtypeStruct((t, d), F32),
        compiler_params=_params("arbitrary"),
    )(pos, x, route, ys)


def _ssd_kernel(*refs, d_inner):
    prev_ref, state_ref = refs[-3], refs[-2]

    @pl.when(pl.program_id(0) == 0)
    def _():
        prev_ref[...] = jnp.zeros(prev_ref.shape, F32)
        state_ref[...] = jnp.zeros(state_ref.shape, F32)

    def chunk_body(c, carry):
        _ssd_chunk(c, *refs, d_inner=d_inner)
        return carry

    lax.fori_loop(0, refs[0].shape[0] // SSD_CHUNK, chunk_body, 0)


def _ssd_chunk(c, xbc_ref, z_ref, dt_ref, cw_ref, cb_ref, dtb_ref, alog_ref, dsk_ref, nw_ref,
               o_ref, prev_ref, state_ref, y_ref, *, d_inner):
    L = SSD_CHUNK
    gn = SSD_GROUPS * SSD_D_STATE
    n_pairs = d_inner // LANES
    pairs_per_group = n_pairs // SSD_GROUPS
    rows = pl.ds(pl.multiple_of(c * L, L), L)

    x = xbc_ref[rows, :]
    xp = prev_ref[...]
    row1 = lax.broadcasted_iota(I32, (SUBLANES, 1), 0)
    cw = cw_ref[...]
    acc = cb_ref[...] + cw[SSD_CONV - 1:SSD_CONV] * x
    for k in range(1, SSD_CONV):
        shifted = pltpu.roll(x, k, 0)
        head = jnp.where(row1 < k, pltpu.roll(xp, k, 0), shifted[:SUBLANES])
        shifted = jnp.concatenate([head, shifted[SUBLANES:]], axis=0)
        acc = acc + cw[SSD_CONV - 1 - k:SSD_CONV - k] * shifted
    prev_ref[...] = x[L - SUBLANES:]
    xc = acc * _sigmoid(acc)

    dtr = dt_ref[rows, :] + dtb_ref[...]
    dtv = jnp.maximum(dtr, 0.0) + jnp.log1p(jnp.exp(-jnp.abs(dtr)))
    da = dtv * (-jnp.exp(alog_ref[...]))
    rowh = lax.broadcasted_iota(I32, da.shape, 0)
    acs = da
    s = 1
    while s < L:
        acs = acs + jnp.where(rowh >= s, pltpu.roll(acs, s, 0), 0.0)
        s *= 2
    acs_t = acs.T
    tot = acs[L - 1:L, :]
    decay_to_end = jnp.exp(tot - acs)
    decay_from_start = jnp.exp(acs)
    tot_col = jnp.exp(acs_t[:, L - 1:L])

    lane = lax.broadcasted_iota(I32, (L, LANES), 1)
    rowl = lax.broadcasted_iota(I32, (L, LANES), 0)
    left = lane < SSD_HEAD_DIM
    causal = rowl >= lane

    def per_head(a, h0):
        return jnp.where(left, a[:, h0:h0 + 1], a[:, h0 + 1:h0 + 2])

    for g in range(SSD_GROUPS):
        bg = xc[:, d_inner + g * SSD_D_STATE:d_inner + (g + 1) * SSD_D_STATE].astype(BF16)
        cg = xc[:, d_inner + gn + g * SSD_D_STATE:d_inner + gn + (g + 1) * SSD_D_STATE].astype(BF16)
        cbm = lax.dot_general(cg, bg, (((1,), (1,)), ((), ())), preferred_element_type=F32)
        for jp in range(pairs_per_group):
            pj = g * pairs_per_group + jp
            h0 = 2 * pj
            xs_pair = xc[:, pj * LANES:(pj + 1) * LANES]
            xd = xs_pair * per_head(dtv, h0)
            y = jnp.zeros((L, LANES), F32)
            for half, hh in enumerate((h0, h0 + 1)):
                diff = acs[:, hh:hh + 1] - acs_t[hh:hh + 1, :]
                dec = jnp.exp(jnp.where(causal, diff, MASK_VALUE))
                in_half = left if half == 0 else jnp.logical_not(left)
                y = y + jnp.dot((cbm * dec).astype(BF16),
                                jnp.where(in_half, xd, 0.0).astype(BF16),
                                preferred_element_type=F32)
            s_old = state_ref[pj]
            y_off = lax.dot_general(cg, s_old.astype(BF16), (((1,), (1,)), ((), ())),
                                    preferred_element_type=F32)
            y = y + y_off * per_head(decay_from_start, h0)
            s_new = lax.dot_general((xd * per_head(decay_to_end, h0)).astype(BF16), bg,
                                    (((0,), (0,)), ((), ())), preferred_element_type=F32)
            carry = jnp.where(rowl < SSD_HEAD_DIM, tot_col[h0:h0 + 1, :], tot_col[h0 + 1:h0 + 2, :])
            state_ref[pj] = carry * s_old + s_new
            y_ref[:, pj * LANES:(pj + 1) * LANES] = y + dsk_ref[:, pj * LANES:(pj + 1) * LANES] * xs_pair

    z = z_ref[rows, :].astype(F32)
    yg = y_ref[...] * (z * _sigmoid(z))
    gw = d_inner // SSD_GROUPS
    for g in range(SSD_GROUPS):
        blk = yg[:, g * gw:(g + 1) * gw]
        o_ref[rows, g * gw:(g + 1) * gw] = _rms_rows(blk, nw_ref[:, g * gw:(g + 1) * gw]).astype(o_ref.dtype)


def ssd_core(xbc, z, dt, conv_w, conv_b, dt_bias, a_log, d_skip, norm_w, chunks_per_step=4):
    t = xbc.shape[0]
    d_inner = z.shape[1]
    conv_dim = xbc.shape[1]
    heads = d_inner // SSD_HEAD_DIM
    assert t % SSD_CHUNK == 0 and heads <= LANES
    while (t // SSD_CHUNK) % chunks_per_step:
        chunks_per_step //= 2
    L = SSD_CHUNK * chunks_per_step

    def lanes(v):
        return jnp.zeros((1, LANES), F32).at[0, :heads].set(v.astype(F32))

    row = lambda i: (i, 0)
    fixed = lambda i: (0, 0)
    return pl.pallas_call(
        functools.partial(_ssd_kernel, d_inner=d_inner),
        name="ssd",
        grid=(t // L,),
        in_specs=[
            pl.BlockSpec((L, conv_dim), row),
            pl.BlockSpec((L, d_inner), row),
            pl.BlockSpec((L, LANES), row),
            pl.BlockSpec((SSD_CONV, conv_dim), fixed),
            pl.BlockSpec((1, conv_dim), fixed),
            pl.BlockSpec((1, LANES), fixed),
            pl.BlockSpec((1, LANES), fixed),
            pl.BlockSpec((1, d_inner), fixed),
            pl.BlockSpec((1, d_inner), fixed),
        ],
        out_specs=pl.BlockSpec((L, d_inner), row),
        out_shape=jax.ShapeDtypeStruct((t, d_inner), BF16),
        scratch_shapes=[pltpu.VMEM((SUBLANES, conv_dim), F32),
                        pltpu.VMEM((d_inner // LANES, LANES, SSD_D_STATE), F32),
                        pltpu.VMEM((SSD_CHUNK, d_inner), F32)],
        compiler_params=_params("arbitrary"),
    )(xbc, z, dt, conv_w.astype(F32), conv_b.reshape(1, conv_dim).astype(F32),
      lanes(dt_bias), lanes(a_log),
      jnp.repeat(d_skip.astype(F32), SSD_HEAD_DIM).reshape(1, d_inner),
      norm_w.reshape(1, d_inner).astype(F32))


def lambda_init_for(layer_idx):
    return 0.8 - 0.6 * math.exp(-0.3 * layer_idx)


def _ssd_inproj_kernel(x_ref, g_ref, w_hbm, wdt_ref, z_ref, xbc_ref, dt_ref,
                       hn_ref, wres_ref, stg_ref, sem, *, layer, tn, d_inner):
    nb = wres_ref.shape[0]

    def block_copy(jb, slot):
        c0 = pl.multiple_of(jb * tn, tn)
        return pltpu.make_async_copy(w_hbm.at[layer, :, pl.ds(c0, tn)], stg_ref.at[slot], sem.at[slot])

    @pl.when(pl.program_id(0) == 0)
    def _():
        block_copy(0, 0).start()

        def body(jb, carry):
            slot = jb % 2
            block_copy(jb, slot).wait()

            @pl.when(jb + 1 < nb)
            def _():
                block_copy(jb + 1, 1 - slot).start()

            wres_ref[jb] = stg_ref[slot].astype(BF16)
            return carry

        lax.fori_loop(0, nb, body, 0)

    hn_ref[...] = _rms_rows(x_ref[...], g_ref[...]).astype(BF16)
    hn = hn_ref[...]
    for jb in range(nb):
        y = jnp.dot(hn, wres_ref[jb], preferred_element_type=F32)
        c0 = jb * tn
        if c0 < d_inner:
            z_ref[:, c0:c0 + tn] = y.astype(z_ref.dtype)
        else:
            xbc_ref[:, c0 - d_inner:c0 - d_inner + tn] = y
    dt_ref[...] = jnp.dot(hn, wdt_ref[...], preferred_element_type=F32)


def ssd_inproj(x, gain, w_in_all, layer, d_inner, conv_dim, heads, tm=512, tn=512):
    t, d = x.shape
    tm = min(tm, t)
    n_main = d_inner + conv_dim
    assert t % tm == 0 and d_inner % tn == 0 and conv_dim % tn == 0 and heads <= LANES
    w_dt = jnp.zeros((d, LANES), BF16).at[:, :heads].set(w_in_all[layer, :, n_main:].astype(BF16))
    row = lambda i: (i, 0)
    fixed = lambda i: (0, 0)
    return pl.pallas_call(
        functools.partial(_ssd_inproj_kernel, layer=layer, tn=tn, d_inner=d_inner),
        name="ssd_inproj",
        grid=(t // tm,),
        in_specs=[pl.BlockSpec((tm, d), row),
                  pl.BlockSpec((1, d), fixed),
                  pl.BlockSpec(memory_space=pl.ANY),
                  pl.BlockSpec((d, LANES), fixed)],
        out_specs=[pl.BlockSpec((tm, d_inner), row),
                   pl.BlockSpec((tm, conv_dim), row),
                   pl.BlockSpec((tm, LANES), row)],
        out_shape=[jax.ShapeDtypeStruct((t, d_inner), BF16),
                   jax.ShapeDtypeStruct((t, conv_dim), F32),
                   jax.ShapeDtypeStruct((t, LANES), F32)],
        scratch_shapes=[pltpu.VMEM((tm, d), BF16),
                        pltpu.VMEM((n_main // tn, d, tn), BF16),
                        pltpu.VMEM((2, d, tn), F32),
                        pltpu.SemaphoreType.DMA((2,))],
        compiler_params=_params("arbitrary"),
    )(x, gain.reshape(1, d), w_in_all, w_dt)


def ssd_layer(xt, gain, w_in_all, layer, conv_w, conv_b, dt_bias, a_log, d_skip, norm_w, w_out):
    d_inner = w_out.shape[0]
    conv_dim = conv_w.shape[1]
    heads = dt_bias.shape[0]
    z, xbc, dt = ssd_inproj(xt, gain, w_in_all, layer, d_inner, conv_dim, heads)
    y = ssd_core(xbc, z, dt, conv_w, conv_b, dt_bias, a_log, d_skip, norm_w)
    return proj_residual(xt, y, w_out.astype(BF16))


def moe_layer(xt, gain, w_router, w_gate, w_up, w_down, layer, tmg=512):
    t, d = xt.shape
    ne = w_router.shape[1]
    tmg = min(tmg, t)
    route, counts = router(xt, gain, w_router)
    counts = counts[0, :ne].astype(I32)
    padded = (counts + tmg - 1) // tmg * tmg
    ends = jnp.cumsum(padded)
    starts = ends - padded
    experts = route[:, R_E1:R_E2 + 1].astype(I32)
    ranks = route[:, R_RANK1:R_RANK2 + 1].astype(I32)
    pos = (jnp.take(starts, experts) + ranks).reshape(-1)
    n_rows = TOP_K * t + ne * tmg
    n_tiles = n_rows // tmg
    n_used = (ends[-1] // tmg).astype(I32)
    tile_row = jnp.minimum(jnp.arange(n_tiles, dtype=I32), n_used - 1) * tmg
    tile_expert = jnp.sum(tile_row[:, None] >= ends[None, :], axis=1).astype(I32)
    first = jnp.concatenate([jnp.ones((1,), I32), (tile_expert[1:] != tile_expert[:-1]).astype(I32)])
    xs = dispatch(xt, gain, pos, n_rows)
    ys = ffn_tiles(xs, tile_expert, first, n_used.reshape(1), w_gate, w_up, w_down, layer, tm=tmg)
    return combine(xt, route, ys, pos)


def kernel(x, rel_bias, norm_mix, norm_ffn, attn_w_in, attn_q_gain, attn_k_gain, attn_lambda, attn_subln, attn_w_out, ffn_w_gate, ffn_w_up, ffn_w_down, ssd_w_in, ssd_conv_w, ssd_conv_b, ssd_dt_bias, ssd_a_log, ssd_d, ssd_norm, ssd_w_out, moe_router, moe_w_gate, moe_w_up, moe_w_down):
    b, s, d = x.shape
    t = b * s
    assert b == 1
    depth = norm_mix.shape[0]
    xt = x.reshape(t, d).astype(F32)
    tq = min(512, t)
    bias = bias_tiles(rel_bias, tq)
    for i in range(depth):
        j = i // 2
        if i % 2 == 0:
            qt, k, vt = attn_inproj(xt, norm_mix[i], attn_w_in[j].astype(BF16), attn_q_gain[j], attn_k_gain[j])
            o = diff_attention_core(qt, k, vt, bias, attn_lambda[j], attn_subln[j], lambda_init_for(i), tq)
            xt = proj_residual(xt, o, attn_w_out[j].astype(BF16))
            xt = ffn_residual(xt, norm_ffn[i], ffn_w_gate, ffn_w_up, ffn_w_down, j)
        else:
            xt = ssd_layer(xt, norm_mix[i], ssd_w_in, j, ssd_conv_w[j], ssd_conv_b[j], ssd_dt_bias[j],
                           ssd_a_log[j], ssd_d[j], ssd_norm[j], ssd_w_out[j])
            xt = moe_layer(xt, norm_ffn[i], moe_router[j], moe_w_gate, moe_w_up, moe_w_down, j)
    return xt.reshape(b, s, d)
```

```python
import functools
import math

import jax
import jax.numpy as jnp
from jax import lax
from jax.experimental import pallas as pl
from jax.experimental.pallas import tpu as pltpu

F32 = jnp.float32
BF16 = jnp.bfloat16
I32 = jnp.int32

NORM_EPS = 1e-6
LANES = 128
SUBLANES = 8
MASK_VALUE = -1e30
LOG2E = math.log2(math.e)
ONES_ROWS = 16

ATTN_HEAD_DIM = 64
NUM_BUCKETS = 32
MAX_EXACT = NUM_BUCKETS // 2
MAX_DISTANCE = 128
SSD_HEAD_DIM = 64
SSD_GROUPS = 4
SSD_D_STATE = 128
SSD_CONV = 4
SSD_CHUNK = 128
TOP_K = 2

VMEM_LIMIT = 56 * 1024 * 1024


def _params(*sem):
    return pltpu.CompilerParams(dimension_semantics=sem, vmem_limit_bytes=VMEM_LIMIT)


def _sigmoid(x):
    return 1.0 / (1.0 + jnp.exp(-x))


def _rms_rows(x, gain):
    ms = jnp.mean(x * x, axis=-1, keepdims=True)
    return x * lax.rsqrt(ms + NORM_EPS) * gain


def _attn_inproj_kernel(x_ref, g_ref, w_ref, qkg_ref, bd_ref, qt_ref, k_ref, vt_ref, hn_ref):
    hn_ref[...] = _rms_rows(x_ref[...], g_ref[...]).astype(BF16)
    hn = hn_ref[...]
    width = k_ref.shape[1]
    bd = bd_ref[...]
    w2 = bd.shape[0]

    def head_norm(y, gain_row):
        y2 = (y * y).astype(BF16)
        ms = jnp.concatenate(
            [jnp.dot(y2[:, c * w2:(c + 1) * w2], bd, preferred_element_type=F32)
             for c in range(width // w2)], axis=1)
        return y * lax.rsqrt(ms + NORM_EPS) * gain_row

    q = jnp.dot(hn, w_ref[:, 0:width], preferred_element_type=F32)
    qt_ref[...] = head_norm(q, qkg_ref[0:1, :]).T.astype(BF16)
    k = jnp.dot(hn, w_ref[:, width:2 * width], preferred_element_type=F32)
    k_ref[...] = head_norm(k, qkg_ref[1:2, :]).astype(BF16)
    v = jnp.dot(hn, w_ref[:, 2 * width:3 * width], preferred_element_type=F32)
    vt = v.astype(BF16).T
    hd = 2 * ATTN_HEAD_DIM
    ones = jnp.ones((ONES_ROWS, vt.shape[1]), BF16)
    for h in range(width // hd):
        r0 = h * (hd + ONES_ROWS)
        vt_ref[r0:r0 + hd, :] = vt[h * hd:(h + 1) * hd, :]
        vt_ref[r0 + hd:r0 + hd + ONES_ROWS, :] = ones


def attn_inproj(x, gain, w_in, q_gain, k_gain, tm=512):
    t, d = x.shape
    width = w_in.shape[1] // 3
    tm = min(tm, t)
    reps = width // ATTN_HEAD_DIM
    qk_gain = jnp.stack([jnp.tile(q_gain, reps) * (ATTN_HEAD_DIM ** -0.5 * LOG2E),
                         jnp.tile(k_gain, reps)]).astype(F32)
    hd = 2 * ATTN_HEAD_DIM
    vt_rows = (width // hd) * (hd + ONES_ROWS)
    bw = 256
    grp = jnp.arange(bw) // ATTN_HEAD_DIM
    bd = jnp.where(grp[:, None] == grp[None, :], 1.0 / ATTN_HEAD_DIM, 0.0).astype(BF16)
    return pl.pallas_call(
        _attn_inproj_kernel,
        name="attn_inproj",
        grid=(t // tm,),
        in_specs=[
            pl.BlockSpec((tm, d), lambda i: (i, 0)),
            pl.BlockSpec((1, d), lambda i: (0, 0)),
            pl.BlockSpec((d, 3 * width), lambda i: (0, 0)),
            pl.BlockSpec((2, width), lambda i: (0, 0)),
            pl.BlockSpec((bw, bw), lambda i: (0, 0)),
        ],
        out_specs=[pl.BlockSpec((width, tm), lambda i: (0, i)),
                   pl.BlockSpec((tm, width), lambda i: (i, 0)),
                   pl.BlockSpec((vt_rows, tm), lambda i: (0, i))],
        out_shape=[jax.ShapeDtypeStruct((width, t), BF16),
                   jax.ShapeDtypeStruct((t, width), BF16),
                   jax.ShapeDtypeStruct((vt_rows, t), BF16)],
        scratch_shapes=[pltpu.VMEM((tm, d), BF16)],
        compiler_params=_params("parallel"),
    )(x, gain.reshape(1, d), w_in, qk_gain, bd)


def _bias_tiles_kernel(tab_ref, o_ref, *, tq):
    h = pl.program_id(0)
    r = lax.broadcasted_iota(I32, (tq, tq), 0)
    c = lax.broadcasted_iota(I32, (tq, tq), 1)
    far = tab_ref[NUM_BUCKETS - 1, h]
    for d in range(2):
        dist = c - r + d * tq
        n = jnp.maximum(dist, 0)
        nf = jnp.maximum(n, 1).astype(F32)
        large = MAX_EXACT + (jnp.log(nf / MAX_EXACT) / math.log(MAX_DISTANCE / MAX_EXACT)
                             * (NUM_BUCKETS - MAX_EXACT)).astype(I32)
        large = jnp.minimum(large, NUM_BUCKETS - 1)
        bucket = jnp.where(n < MAX_EXACT, n, large)
        bias = jnp.zeros((tq, tq), F32)
        for b in range(NUM_BUCKETS):
            bias = jnp.where(bucket == b, tab_ref[b, h], bias)
        o_ref[0, d] = jnp.where(dist >= 0, (bias - far) * LOG2E, MASK_VALUE)


def bias_tiles(rel_bias, tq):
    nb, h = rel_bias.shape
    return pl.pallas_call(
        functools.partial(_bias_tiles_kernel, tq=tq),
        name="bias_tiles",
        grid=(h,),
        in_specs=[pl.BlockSpec(memory_space=pltpu.SMEM)],
        out_specs=pl.BlockSpec((1, 2, tq, tq), lambda i: (i, 0, 0, 0)),
        out_shape=jax.ShapeDtypeStruct((h, 2, tq, tq), F32),
        compiler_params=_params("parallel"),
    )(rel_bias.astype(F32))


def _attn_kernel(*refs, tq, cw, lambda_init):
    n_q = refs[1].shape[0] // tq

    def q_body(qi, carry):
        _attn_q_block(qi, *refs, tq=tq, cw=cw, lambda_init=lambda_init)
        return carry

    lax.fori_loop(0, n_q, q_body, 0)


def _attn_q_block(qi, qt_ref, k_ref, vt_ref, bias_ref, lam_ref, sub_ref, o_ref,
                  m_ref, smax_ref, acc_ref, qbd_ref, s_ref, *, tq, cw, lambda_init):
    d = ATTN_HEAD_DIM
    hd = 2 * d
    q0 = pl.multiple_of(qi * tq, tq)
    qt = qt_ref[:, pl.ds(q0, tq)]
    row = lax.broadcasted_iota(I32, qt.shape, 0)
    zero = jnp.zeros_like(qt)
    qbd_ref[:, :tq] = jnp.where(row < d, qt, zero)
    qbd_ref[:, tq:] = jnp.where(row >= d, qt, zero)

    m_ref[...] = jnp.full(m_ref.shape, MASK_VALUE, F32)
    acc_ref[...] = jnp.zeros(acc_ref.shape, F32)

    n_chunks = 2 * tq // cw

    def block_start(j):
        return pl.multiple_of(j * tq, tq)

    def scores(j, c):
        kb = k_ref[pl.ds(block_start(j), tq), :]
        s = jnp.dot(kb, qbd_ref[:, c * cw:(c + 1) * cw], preferred_element_type=F32)
        s_ref[:, c * cw:(c + 1) * cw] = s
        smax_ref[:, c * cw:(c + 1) * cw] = jnp.max(s, axis=0, keepdims=True)

    def stage(j, bias_idx, has_next):
        vtb = vt_ref[:, pl.ds(block_start(j), tq)]
        for c in range(n_chunks):
            cols = slice(c * cw, (c + 1) * cw)
            s = s_ref[:, cols]
            if bias_idx is None:
                s_max = smax_ref[:, cols]
            else:
                b0 = (c * cw) % tq
                s = s + bias_ref[0, bias_idx, :, b0:b0 + cw]
                s_max = jnp.max(s, axis=0, keepdims=True)
            m_prev = m_ref[:, cols]
            m_new = jnp.maximum(m_prev, s_max)
            p = jnp.exp2(s - m_new)
            if has_next:
                scores(j + 1, c)
            alpha = jnp.exp2(m_prev - m_new)
            acc_ref[:, cols] = alpha * acc_ref[:, cols] + jnp.dot(vtb, p.astype(BF16),
                                                                  preferred_element_type=F32)
            m_ref[:, cols] = m_new

    for c in range(n_chunks):
        scores(0, c)

    n_far = jnp.maximum(qi - 1, 0)

    def far_quad(i, carry):
        for u in range(4):
            stage(4 * i + u, None, True)
        return carry

    lax.fori_loop(0, n_far // 4, far_quad, 0)

    @pl.when(n_far % 4 >= 2)
    def _():
        first = n_far // 4 * 4
        stage(first, None, True)
        stage(first + 1, None, True)

    def finish(blocks):
        for idx, (j, bias_idx) in enumerate(blocks):
            stage(j, bias_idx, idx + 1 < len(blocks))
        o = acc_ref[0:hd, :] / acc_ref[hd:hd + 1, :]
        lp = lam_ref[...]
        lam = (jnp.exp(jnp.sum(lp[0:1] * lp[1:2], axis=-1, keepdims=True))
               - jnp.exp(jnp.sum(lp[2:3] * lp[3:4], axis=-1, keepdims=True)) + lambda_init)
        o = o[:, :tq] - lam * o[:, tq:]
        ms = jnp.mean(o * o, axis=0, keepdims=True)
        o = o * lax.rsqrt(ms + NORM_EPS) * sub_ref[...] * (1.0 - lambda_init)
        o_ref[pl.ds(q0, tq), :] = o.T.astype(o_ref.dtype)

    @pl.when(qi == 0)
    def _():
        finish([(qi, 0)])

    @pl.when((qi >= 1) & (n_far % 2 == 0))
    def _():
        finish([(qi - 1, 1), (qi, 0)])

    @pl.when(n_far % 2 == 1)
    def _():
        finish([(n_far - 1, None), (qi - 1, 1), (qi, 0)])


def diff_attention_core(qt, k, vt, bias, lam_params, subln_w, lambda_init, tq, cw=512):
    t, width = k.shape
    hd = 2 * ATTN_HEAD_DIM
    heads = width // hd
    vrows = hd + ONES_ROWS
    assert t % tq == 0 and tq >= MAX_DISTANCE and vt.shape[0] == heads * vrows
    return pl.pallas_call(
        functools.partial(_attn_kernel, tq=tq, cw=min(cw, tq), lambda_init=lambda_init),
        name="diff_attn",
        grid=(heads,),
        in_specs=[
            pl.BlockSpec((hd, t), lambda h: (h, 0)),
            pl.BlockSpec((t, hd), lambda h: (0, h)),
            pl.BlockSpec((vrows, t), lambda h: (h, 0)),
            pl.BlockSpec((1, 2, tq, tq), lambda h: (h, 0, 0, 0)),
            pl.BlockSpec((4, ATTN_HEAD_DIM), lambda h: (0, 0)),
            pl.BlockSpec((hd, 1), lambda h: (0, 0)),
        ],
        out_specs=pl.BlockSpec((t, hd), lambda h: (0, h)),
        out_shape=jax.ShapeDtypeStruct((t, width), BF16),
        scratch_shapes=[pltpu.VMEM((1, 2 * tq), F32), pltpu.VMEM((1, 2 * tq), F32),
                        pltpu.VMEM((vrows, 2 * tq), F32), pltpu.VMEM((hd, 2 * tq), BF16),
                        pltpu.VMEM((tq, 2 * tq), F32)],
        compiler_params=_params("parallel"),
    )(qt, k, vt, bias, lam_params.astype(F32), subln_w.reshape(hd, 1).astype(F32))


def _proj_residual_kernel(x_ref, a_ref, w_ref, o_ref):
    o_ref[...] = x_ref[...] + jnp.dot(a_ref[...], w_ref[...], preferred_element_type=F32)


def proj_residual(x, a, w, tm=1024):
    t, d = x.shape
    k = a.shape[1]
    tm = min(tm, t)
    return pl.pallas_call(
        _proj_residual_kernel,
        name="proj_residual",
        grid=(t // tm,),
        in_specs=[
            pl.BlockSpec((tm, d), lambda i: (i, 0)),
            pl.BlockSpec((tm, k), lambda i: (i, 0)),
            pl.BlockSpec((k, d), lambda i: (0, 0)),
        ],
        out_specs=pl.BlockSpec((tm, d), lambda i: (i, 0)),
        out_shape=jax.ShapeDtypeStruct((t, d), F32),
        compiler_params=_params("parallel"),
    )(x, a, w)


def _swiglu_step(hn, wg, wu, wd):
    gate = jnp.dot(hn, wg, preferred_element_type=F32)
    up = jnp.dot(hn, wu, preferred_element_type=F32)
    act = gate * _sigmoid(gate) * up
    return jnp.dot(act.astype(BF16), wd, preferred_element_type=F32)


def _ffn_kernel(te_ref, first_ref, nu_ref, *refs, layer, normalize, tf):
    if normalize:
        x_ref, g_ref, wg_hbm, wu_hbm, wd_hbm, o_ref = refs[:6]
    else:
        x_ref, wg_hbm, wu_hbm, wd_hbm, o_ref = refs[:5]
    hn_ref, act_ref, wg_res, wu_res, wd_res, wg_stg, wu_stg, wd_stg, sem = refs[-9:]
    i = pl.program_id(0)
    nf = wg_res.shape[0]
    e = te_ref[i]

    def block_copies(jb, slot):
        f0 = jb * tf
        return (pltpu.make_async_copy(wg_hbm.at[layer, e, :, pl.ds(f0, tf)], wg_stg.at[slot], sem.at[0, slot]),
                pltpu.make_async_copy(wu_hbm.at[layer, e, :, pl.ds(f0, tf)], wu_stg.at[slot], sem.at[1, slot]),
                pltpu.make_async_copy(wd_hbm.at[layer, e, pl.ds(f0, tf), :], wd_stg.at[slot], sem.at[2, slot]))

    def tile(stream_weights):
        x = x_ref[...]
        if normalize:
            hn_ref[...] = _rms_rows(x, g_ref[...]).astype(BF16)
        else:
            hn_ref[...] = x.astype(BF16)
        hn = hn_ref[...]
        if stream_weights:
            for cp in block_copies(0, 0):
                cp.start()
        for jb in range(nf):
            if stream_weights:
                slot = jb % 2
                for cp in block_copies(jb, slot):
                    cp.wait()
                if jb + 1 < nf:
                    for cp in block_copies(jb + 1, 1 - slot):
                        cp.start()
                wg_res[jb] = wg_stg[slot].astype(BF16)
                wu_res[jb] = wu_stg[slot].astype(BF16)
                wd_res[jb * tf:(jb + 1) * tf, :] = wd_stg[slot].astype(BF16)
            gate = jnp.dot(hn, wg_res[jb], preferred_element_type=F32)
            up = jnp.dot(hn, wu_res[jb], preferred_element_type=F32)
            act_ref[:, jb * tf:(jb + 1) * tf] = (gate * _sigmoid(gate) * up).astype(BF16)
        y = jnp.dot(act_ref[...], wd_res[...], preferred_element_type=F32)
        o_ref[...] = x + y if normalize else y

    @pl.when(i < nu_ref[0])
    def _():
        @pl.when(first_ref[i] == 1)
        def _():
            tile(True)

        @pl.when(first_ref[i] != 1)
        def _():
            tile(False)

    @pl.when(i >= nu_ref[0])
    def _():
        o_ref[...] = jnp.zeros(o_ref.shape, F32)


def ffn_tiles(x, tile_expert, first, n_used, w_gate, w_up, w_down, layer, gain=None, tm=512, tf=512):
    rows, d = x.shape
    f = w_gate.shape[3]
    nf = f // tf
    assert rows % tm == 0 and f % tf == 0
    normalize = gain is not None

    def row_tile(i, te, fr, nu):
        return (jnp.maximum(jnp.minimum(i, nu[0] - 1), 0), 0)

    in_specs = [pl.BlockSpec((tm, d), row_tile)]
    args = [x]
    if normalize:
        in_specs.append(pl.BlockSpec((1, d), lambda i, te, fr, nu: (0, 0)))
        args.append(gain.reshape(1, d))
    in_specs += [pl.BlockSpec(memory_space=pl.ANY)] * 3
    return pl.pallas_call(
        functools.partial(_ffn_kernel, layer=layer, normalize=normalize, tf=tf),
        name="ffn",
        grid_spec=pltpu.PrefetchScalarGridSpec(
            num_scalar_prefetch=3,
            grid=(rows // tm,),
            in_specs=in_specs,
            out_specs=pl.BlockSpec((tm, d), lambda i, te, fr, nu: (i, 0)),
            scratch_shapes=[pltpu.VMEM((tm, d), BF16), pltpu.VMEM((tm, f), BF16),
                            pltpu.VMEM((nf, d, tf), BF16), pltpu.VMEM((nf, d, tf), BF16),
                            pltpu.VMEM((f, d), BF16),
                            pltpu.VMEM((2, d, tf), F32), pltpu.VMEM((2, d, tf), F32),
                            pltpu.VMEM((2, tf, d), F32),
                            pltpu.SemaphoreType.DMA((3, 2))],
        ),
        out_shape=jax.ShapeDtypeStruct((rows, d), F32),
        compiler_params=_params("arbitrary"),
    )(tile_expert, first, n_used, *args, w_gate, w_up, w_down)


def ffn_residual(x, gain, w_gate, w_up, w_down, layer, tm=512):
    n_tiles = x.shape[0] // min(tm, x.shape[0])
    tiles = jnp.arange(n_tiles, dtype=I32)
    return ffn_tiles(x, jnp.zeros((n_tiles,), I32), (tiles == 0).astype(I32),
                     jnp.full((1,), n_tiles, I32), w_gate[:, None], w_up[:, None], w_down[:, None],
                     layer, gain=gain, tm=min(tm, x.shape[0]))


R_E1, R_E2, R_RANK1, R_RANK2, R_W1, R_W2 = range(6)


def _router_kernel(x_ref, g_ref, r_ref, tri_ref, route_ref, cnt_ref, base_ref, *, n_experts):
    @pl.when(pl.program_id(0) == 0)
    def _():
        base_ref[...] = jnp.zeros(base_ref.shape, F32)

    hn = _rms_rows(x_ref[...], g_ref[...])
    hn_hi = hn.astype(BF16)
    hn_lo = (hn - hn_hi.astype(F32)).astype(BF16)
    logits = (jnp.dot(hn_hi, r_ref[0], preferred_element_type=F32)
              + jnp.dot(hn_hi, r_ref[1], preferred_element_type=F32)
              + jnp.dot(hn_lo, r_ref[0], preferred_element_type=F32))
    lane = lax.broadcasted_iota(I32, logits.shape, 1)
    lg = jnp.where(lane < n_experts, logits, MASK_VALUE)
    m1 = jnp.max(lg, axis=-1, keepdims=True)
    i1 = jnp.min(jnp.where(lg == m1, lane, LANES), axis=-1, keepdims=True)
    lg2 = jnp.where(lane == i1, MASK_VALUE, lg)
    m2 = jnp.max(lg2, axis=-1, keepdims=True)
    i2 = jnp.min(jnp.where(lg2 == m2, lane, LANES), axis=-1, keepdims=True)
    e2 = jnp.exp(m2 - m1)
    w1 = 1.0 / (1.0 + e2)
    w2 = e2 / (1.0 + e2)

    sel1 = lane == i1
    sel2 = lane == i2
    mask = jnp.where(sel1 | sel2, 1.0, 0.0)
    rank = base_ref[...] + jnp.dot(tri_ref[...], mask.astype(BF16), preferred_element_type=F32)
    rank1 = jnp.sum(jnp.where(sel1, rank, 0.0), axis=-1, keepdims=True)
    rank2 = jnp.sum(jnp.where(sel2, rank, 0.0), axis=-1, keepdims=True)
    base_ref[...] += jnp.sum(mask, axis=0, keepdims=True)

    rec = jnp.zeros(logits.shape, F32)
    for slot, val in ((R_E1, i1.astype(F32)), (R_E2, i2.astype(F32)), (R_RANK1, rank1),
                      (R_RANK2, rank2), (R_W1, w1), (R_W2, w2)):
        rec = jnp.where(lane == slot, val, rec)
    route_ref[...] = rec
    cnt_ref[...] = jnp.broadcast_to(base_ref[...], cnt_ref.shape)


def router(x, gain, w_router, tm=1024):
    t, d = x.shape
    ne = w_router.shape[1]
    tm = min(tm, t)
    r_pad = jnp.zeros((d, LANES), F32).at[:, :ne].set(w_router.astype(F32))
    r_hi = r_pad.astype(BF16)
    r_split = jnp.stack([r_hi, (r_pad - r_hi.astype(F32)).astype(BF16)])
    idx = jnp.arange(tm)
    tri = (idx[None, :] < idx[:, None]).astype(BF16)
    return pl.pallas_call(
        functools.partial(_router_kernel, n_experts=ne),
        name="router",
        grid=(t // tm,),
        in_specs=[pl.BlockSpec((tm, d), lambda i: (i, 0)),
                  pl.BlockSpec((1, d), lambda i: (0, 0)),
                  pl.BlockSpec((2, d, LANES), lambda i: (0, 0, 0)),
                  pl.BlockSpec((tm, tm), lambda i: (0, 0))],
        out_specs=[pl.BlockSpec((tm, LANES), lambda i: (i, 0)),
                   pl.BlockSpec((8, LANES), lambda i: (0, 0))],
        out_shape=[jax.ShapeDtypeStruct((t, LANES), F32),
                   jax.ShapeDtypeStruct((8, LANES), F32)],
        scratch_shapes=[pltpu.VMEM((1, LANES), F32)],
        compiler_params=_params("arbitrary"),
    )(x, gain.reshape(1, d), r_split, tri)


def _row_copy(src, r, dst, p, sem):
    return pltpu.make_async_copy(src.at[pl.ds(r, 1), :], dst.at[pl.ds(p, 1), :], sem)


def _dispatch_kernel(pos_ref, zt_ref, x_ref, g_ref, xs_ref, hn_ref, zero_ref, sem, zsem, *, tm):
    tmg = zero_ref.shape[0]

    def zero_tile(z):
        r0 = pl.multiple_of(zt_ref[z], tmg)
        return pltpu.make_async_copy(zero_ref, xs_ref.at[pl.ds(r0, tmg), :], zsem)

    @pl.when(pl.program_id(0) == 0)
    def _():
        zero_ref[...] = jnp.zeros(zero_ref.shape, F32)
        for z in range(zt_ref.shape[0]):
            @pl.when(zt_ref[z] >= 0)
            def _():
                zero_tile(z).start()
        for z in range(zt_ref.shape[0]):
            @pl.when(zt_ref[z] >= 0)
            def _():
                zero_tile(z).wait()

    base = pl.program_id(0) * tm
    hn_ref[...] = _rms_rows(x_ref[...], g_ref[...])

    def issue(r, carry):
        for k in range(TOP_K):
            _row_copy(hn_ref, r, xs_ref, pos_ref[TOP_K * (base + r) + k], sem).start()
        return carry

    lax.fori_loop(0, tm, issue, 0, unroll=8)

    def drain(r, carry):
        for k in range(TOP_K):
            _row_copy(hn_ref, r, xs_ref, pos_ref[TOP_K * (base + r) + k], sem).wait()
        return carry

    lax.fori_loop(0, tm, drain, 0, unroll=8)


def dispatch(x, gain, pos, zero_tiles, n_rows, tmg, tm=512):
    t, d = x.shape
    tm = min(tm, t)
    return pl.pallas_call(
        functools.partial(_dispatch_kernel, tm=tm),
        name="dispatch",
        grid_spec=pltpu.PrefetchScalarGridSpec(
            num_scalar_prefetch=2,
            grid=(t // tm,),
            in_specs=[pl.BlockSpec((tm, d), lambda i, pos, zt: (i, 0)),
                      pl.BlockSpec((1, d), lambda i, pos, zt: (0, 0))],
            out_specs=pl.BlockSpec(memory_space=pl.ANY),
            scratch_shapes=[pltpu.VMEM((tm, d), F32), pltpu.VMEM((tmg, d), F32),
                            pltpu.SemaphoreType.DMA(()), pltpu.SemaphoreType.DMA(())],
        ),
        out_shape=jax.ShapeDtypeStruct((n_rows, d), F32),
        compiler_params=_params("arbitrary"),
    )(pos, zero_tiles, x, gain.reshape(1, d))


def _combine_kernel(pos_ref, x_ref, route_ref, ys_ref, o_ref, buf_ref, sem, *, tm):
    base = pl.program_id(0) * tm

    def issue(r, carry):
        for k in range(TOP_K):
            _row_copy(ys_ref, pos_ref[TOP_K * (base + r) + k], buf_ref.at[k], r, sem).start()
        return carry

    lax.fori_loop(0, tm, issue, 0, unroll=8)

    def drain(r, carry):
        for k in range(TOP_K):
            _row_copy(ys_ref, pos_ref[TOP_K * (base + r) + k], buf_ref.at[k], r, sem).wait()
        return carry

    lax.fori_loop(0, tm, drain, 0, unroll=8)

    rec = route_ref[...]
    lane = lax.broadcasted_iota(I32, rec.shape, 1)
    w1 = jnp.sum(jnp.where(lane == R_W1, rec, 0.0), axis=-1, keepdims=True)
    w2 = jnp.sum(jnp.where(lane == R_W2, rec, 0.0), axis=-1, keepdims=True)
    o_ref[...] = x_ref[...] + w1 * buf_ref[0] + w2 * buf_ref[1]


def combine(x, route, ys, pos, tm=512):
    t, d = x.shape
    tm = min(tm, t)
    return pl.pallas_call(
        functools.partial(_combine_kernel, tm=tm),
        name="combine",
        grid_spec=pltpu.PrefetchScalarGridSpec(
            num_scalar_prefetch=1,
            grid=(t // tm,),
            in_specs=[pl.BlockSpec((tm, d), lambda i, pos: (i, 0)),
                      pl.BlockSpec((tm, LANES), lambda i, pos: (i, 0)),
                      pl.BlockSpec(memory_space=pl.ANY)],
            out_specs=pl.BlockSpec((tm, d), lambda i, pos: (i, 0)),
            scratch_shapes=[pltpu.VMEM((TOP_K, tm, d), F32), pltpu.SemaphoreType.DMA(())],
        ),
        out_shape=jax.ShapeDtypeStruct((t, d), F32),
        compiler_params=_params("arbitrary"),
    )(pos, x, route, ys)


def _ssd_kernel(*refs, d_inner):
    prev_ref, state_ref = refs[-3], refs[-2]

    @pl.when(pl.program_id(0) == 0)
    def _():
        prev_ref[...] = jnp.zeros(prev_ref.shape, F32)
        state_ref[...] = jnp.zeros(state_ref.shape, F32)

    def chunk_body(c, carry):
        _ssd_chunk(c, *refs, d_inner=d_inner)
        return carry

    lax.fori_loop(0, refs[0].shape[0] // SSD_CHUNK, chunk_body, 0)


def _ssd_chunk(c, xbc_ref, z_ref, dt_ref, cw_ref, cb_ref, dtb_ref, alog_ref, dsk_ref, nw_ref,
               o_ref, prev_ref, state_ref, y_ref, *, d_inner):
    L = SSD_CHUNK
    gn = SSD_GROUPS * SSD_D_STATE
    n_pairs = d_inner // LANES
    pairs_per_group = n_pairs // SSD_GROUPS
    rows = pl.ds(pl.multiple_of(c * L, L), L)

    x = xbc_ref[rows, :]
    xp = prev_ref[...]
    row1 = lax.broadcasted_iota(I32, (SUBLANES, 1), 0)
    cw = cw_ref[...]
    acc = cb_ref[...] + cw[SSD_CONV - 1:SSD_CONV] * x
    for k in range(1, SSD_CONV):
        shifted = pltpu.roll(x, k, 0)
        head = jnp.where(row1 < k, pltpu.roll(xp, k, 0), shifted[:SUBLANES])
        shifted = jnp.concatenate([head, shifted[SUBLANES:]], axis=0)
        acc = acc + cw[SSD_CONV - 1 - k:SSD_CONV - k] * shifted
    prev_ref[...] = x[L - SUBLANES:]
    xc = acc * _sigmoid(acc)

    dtr = dt_ref[rows, :] + dtb_ref[...]
    dtv = jnp.maximum(dtr, 0.0) + jnp.log1p(jnp.exp(-jnp.abs(dtr)))
    da = dtv * (-jnp.exp(alog_ref[...]))
    rowh = lax.broadcasted_iota(I32, da.shape, 0)
    acs = da
    s = 1
    while s < L:
        acs = acs + jnp.where(rowh >= s, pltpu.roll(acs, s, 0), 0.0)
        s *= 2
    acs_t = acs.T
    tot = acs[L - 1:L, :]
    decay_to_end = jnp.exp(tot - acs)
    decay_from_start = jnp.exp(acs)
    tot_col = jnp.exp(acs_t[:, L - 1:L])

    lane = lax.broadcasted_iota(I32, (L, LANES), 1)
    rowl = lax.broadcasted_iota(I32, (L, LANES), 0)
    left = lane < SSD_HEAD_DIM
    causal = rowl >= lane

    def per_head(a, h0):
        return jnp.where(left, a[:, h0:h0 + 1], a[:, h0 + 1:h0 + 2])

    for g in range(SSD_GROUPS):
        bg = xc[:, d_inner + g * SSD_D_STATE:d_inner + (g + 1) * SSD_D_STATE].astype(BF16)
        cg = xc[:, d_inner + gn + g * SSD_D_STATE:d_inner + gn + (g + 1) * SSD_D_STATE].astype(BF16)
        cbm = lax.dot_general(cg, bg, (((1,), (1,)), ((), ())), preferred_element_type=F32)
        for jp in range(pairs_per_group):
            pj = g * pairs_per_group + jp
            h0 = 2 * pj
            xs_pair = xc[:, pj * LANES:(pj + 1) * LANES]
            xd = xs_pair * per_head(dtv, h0)
            y = jnp.zeros((L, LANES), F32)
            for half, hh in enumerate((h0, h0 + 1)):
                diff = acs[:, hh:hh + 1] - acs_t[hh:hh + 1, :]
                dec = jnp.exp(jnp.where(causal, diff, MASK_VALUE))
                in_half = left if half == 0 else jnp.logical_not(left)
                y = y + jnp.dot((cbm * dec).astype(BF16),
                                jnp.where(in_half, xd, 0.0).astype(BF16),
                                preferred_element_type=F32)
            s_old = state_ref[pj]
            y_off = lax.dot_general(cg, s_old.astype(BF16), (((1,), (1,)), ((), ())),
                                    preferred_element_type=F32)
            y = y + y_off * per_head(decay_from_start, h0)
            s_new = lax.dot_general((xd * per_head(decay_to_end, h0)).astype(BF16), bg,
                                    (((0,), (0,)), ((), ())), preferred_element_type=F32)
            carry = jnp.where(rowl < SSD_HEAD_DIM, tot_col[h0:h0 + 1, :], tot_col[h0 + 1:h0 + 2, :])
            state_ref[pj] = carry * s_old + s_new
            y_ref[:, pj * LANES:(pj + 1) * LANES] = y + dsk_ref[:, pj * LANES:(pj + 1) * LANES] * xs_pair

    z = z_ref[rows, :].astype(F32)
    yg = y_ref[...] * (z * _sigmoid(z))
    gw = d_inner // SSD_GROUPS
    for g in range(SSD_GROUPS):
        blk = yg[:, g * gw:(g + 1) * gw]
        o_ref[rows, g * gw:(g + 1) * gw] = _rms_rows(blk, nw_ref[:, g * gw:(g + 1) * gw]).astype(o_ref.dtype)


def ssd_core(xbc, z, dt, conv_w, conv_b, dt_bias, a_log, d_skip, norm_w, chunks_per_step=4):
    t = xbc.shape[0]
    d_inner = z.shape[1]
    conv_dim = xbc.shape[1]
    heads = d_inner // SSD_HEAD_DIM
    assert t % SSD_CHUNK == 0 and heads <= LANES
    while (t // SSD_CHUNK) % chunks_per_step:
        chunks_per_step //= 2
    L = SSD_CHUNK * chunks_per_step

    def lanes(v):
        return jnp.zeros((1, LANES), F32).at[0, :heads].set(v.astype(F32))

    row = lambda i: (i, 0)
    fixed = lambda i: (0, 0)
    return pl.pallas_call(
        functools.partial(_ssd_kernel, d_inner=d_inner),
        name="ssd",
        grid=(t // L,),
        in_specs=[
            pl.BlockSpec((L, conv_dim), row),
            pl.BlockSpec((L, d_inner), row),
            pl.BlockSpec((L, LANES), row),
            pl.BlockSpec((SSD_CONV, conv_dim), fixed),
            pl.BlockSpec((1, conv_dim), fixed),
            pl.BlockSpec((1, LANES), fixed),
            pl.BlockSpec((1, LANES), fixed),
            pl.BlockSpec((1, d_inner), fixed),
            pl.BlockSpec((1, d_inner), fixed),
        ],
        out_specs=pl.BlockSpec((L, d_inner), row),
        out_shape=jax.ShapeDtypeStruct((t, d_inner), BF16),
        scratch_shapes=[pltpu.VMEM((SUBLANES, conv_dim), F32),
                        pltpu.VMEM((d_inner // LANES, LANES, SSD_D_STATE), F32),
                        pltpu.VMEM((SSD_CHUNK, d_inner), F32)],
        compiler_params=_params("arbitrary"),
    )(xbc, z, dt, conv_w.astype(F32), conv_b.reshape(1, conv_dim).astype(F32),
      lanes(dt_bias), lanes(a_log),
      jnp.repeat(d_skip.astype(F32), SSD_HEAD_DIM).reshape(1, d_inner),
      norm_w.reshape(1, d_inner).astype(F32))


def lambda_init_for(layer_idx):
    return 0.8 - 0.6 * math.exp(-0.3 * layer_idx)


def _ssd_inproj_kernel(x_ref, g_ref, w_hbm, wdt_ref, z_ref, xbc_ref, dt_ref,
                       hn_ref, wres_ref, stg_ref, sem, *, layer, tn, d_inner):
    nb = wres_ref.shape[0]

    def block_copy(jb, slot):
        c0 = pl.multiple_of(jb * tn, tn)
        return pltpu.make_async_copy(w_hbm.at[layer, :, pl.ds(c0, tn)], stg_ref.at[slot], sem.at[slot])

    @pl.when(pl.program_id(0) == 0)
    def _():
        block_copy(0, 0).start()

        def body(jb, carry):
            slot = jb % 2
            block_copy(jb, slot).wait()

            @pl.when(jb + 1 < nb)
            def _():
                block_copy(jb + 1, 1 - slot).start()

            wres_ref[jb] = stg_ref[slot].astype(BF16)
            return carry

        lax.fori_loop(0, nb, body, 0)

    hn_ref[...] = _rms_rows(x_ref[...], g_ref[...]).astype(BF16)
    hn = hn_ref[...]
    for jb in range(nb):
        y = jnp.dot(hn, wres_ref[jb], preferred_element_type=F32)
        c0 = jb * tn
        if c0 < d_inner:
            z_ref[:, c0:c0 + tn] = y.astype(z_ref.dtype)
        else:
            xbc_ref[:, c0 - d_inner:c0 - d_inner + tn] = y
    dt_ref[...] = jnp.dot(hn, wdt_ref[...], preferred_element_type=F32)


def ssd_inproj(x, gain, w_in_all, layer, d_inner, conv_dim, heads, tm=512, tn=512):
    t, d = x.shape
    tm = min(tm, t)
    n_main = d_inner + conv_dim
    assert t % tm == 0 and d_inner % tn == 0 and conv_dim % tn == 0 and heads <= LANES
    w_dt = jnp.zeros((d, LANES), BF16).at[:, :heads].set(w_in_all[layer, :, n_main:].astype(BF16))
    row = lambda i: (i, 0)
    fixed = lambda i: (0, 0)
    return pl.pallas_call(
        functools.partial(_ssd_inproj_kernel, layer=layer, tn=tn, d_inner=d_inner),
        name="ssd_inproj",
        grid=(t // tm,),
        in_specs=[pl.BlockSpec((tm, d), row),
                  pl.BlockSpec((1, d), fixed),
                  pl.BlockSpec(memory_space=pl.ANY),
                  pl.BlockSpec((d, LANES), fixed)],
        out_specs=[pl.BlockSpec((tm, d_inner), row),
                   pl.BlockSpec((tm, conv_dim), row),
                   pl.BlockSpec((tm, LANES), row)],
        out_shape=[jax.ShapeDtypeStruct((t, d_inner), BF16),
                   jax.ShapeDtypeStruct((t, conv_dim), F32),
                   jax.ShapeDtypeStruct((t, LANES), F32)],
        scratch_shapes=[pltpu.VMEM((tm, d), BF16),
                        pltpu.VMEM((n_main // tn, d, tn), BF16),
                        pltpu.VMEM((2, d, tn), F32),
                        pltpu.SemaphoreType.DMA((2,))],
        compiler_params=_params("arbitrary"),
    )(x, gain.reshape(1, d), w_in_all, w_dt)


def ssd_layer(xt, gain, w_in_all, layer, conv_w, conv_b, dt_bias, a_log, d_skip, norm_w, w_out):
    d_inner = w_out.shape[0]
    conv_dim = conv_w.shape[1]
    heads = dt_bias.shape[0]
    z, xbc, dt = ssd_inproj(xt, gain, w_in_all, layer, d_inner, conv_dim, heads)
    y = ssd_core(xbc, z, dt, conv_w, conv_b, dt_bias, a_log, d_skip, norm_w)
    return proj_residual(xt, y, w_out.astype(BF16))


def moe_layer(xt, gain, w_router, w_gate, w_up, w_down, layer, tmg=512):
    t, d = xt.shape
    ne = w_router.shape[1]
    tmg = min(tmg, t)
    route, counts = router(xt, gain, w_router)
    counts = counts[0, :ne].astype(I32)
    padded = (counts + tmg - 1) // tmg * tmg
    ends = jnp.cumsum(padded)
    starts = ends - padded
    experts = route[:, R_E1:R_E2 + 1].astype(I32)
    ranks = route[:, R_RANK1:R_RANK2 + 1].astype(I32)
    pos = (jnp.take(starts, experts) + ranks).reshape(-1)
    n_rows = TOP_K * t + ne * tmg
    n_tiles = n_rows // tmg
    n_used = (ends[-1] // tmg).astype(I32)
    tile_row = jnp.minimum(jnp.arange(n_tiles, dtype=I32), n_used - 1) * tmg
    tile_expert = jnp.sum(tile_row[:, None] >= ends[None, :], axis=1).astype(I32)
    first = jnp.concatenate([jnp.ones((1,), I32), (tile_expert[1:] != tile_expert[:-1]).astype(I32)])
    tail = (n_used + jnp.arange(ne, dtype=I32)) * tmg
    zero_tiles = jnp.concatenate([jnp.where(padded > 0, ends - tmg, -1),
                                  jnp.where(tail < n_rows, tail, -1)]).astype(I32)
    xs = dispatch(xt, gain, pos, zero_tiles, n_rows, tmg)
    ys = ffn_tiles(xs, tile_expert, first, n_used.reshape(1), w_gate, w_up, w_down, layer, tm=tmg)
    return combine(xt, route, ys, pos)


def kernel(x, rel_bias, norm_mix, norm_ffn, attn_w_in, attn_q_gain, attn_k_gain, attn_lambda, attn_subln, attn_w_out, ffn_w_gate, ffn_w_up, ffn_w_down, ssd_w_in, ssd_conv_w, ssd_conv_b, ssd_dt_bias, ssd_a_log, ssd_d, ssd_norm, ssd_w_out, moe_router, moe_w_gate, moe_w_up, moe_w_down):
    b, s, d = x.shape
    t = b * s
    assert b == 1
    depth = norm_mix.shape[0]
    xt = x.reshape(t, d).astype(F32)
    tq = min(512, t)
    bias = bias_tiles(rel_bias, tq)
    for i in range(depth):
        j = i // 2
        if i % 2 == 0:
            qt, k, vt = attn_inproj(xt, norm_mix[i], attn_w_in[j].astype(BF16), attn_q_gain[j], attn_k_gain[j])
            o = diff_attention_core(qt, k, vt, bias, attn_lambda[j], attn_subln[j], lambda_init_for(i), tq)
            xt = proj_residual(xt, o, attn_w_out[j].astype(BF16))
            xt = ffn_residual(xt, norm_ffn[i], ffn_w_gate, ffn_w_up, ffn_w_down, j)
        else:
            xt = ssd_layer(xt, norm_mix[i], ssd_w_in, j, ssd_conv_w[j], ssd_conv_b[j], ssd_dt_bias[j],
                           ssd_a_log[j], ssd_d[j], ssd_norm[j], ssd_w_out[j])
            xt = moe_layer(xt, norm_ffn[i], moe_router[j], moe_w_gate, moe_w_up, moe_w_down, j)
    return xt.reshape(b, s, d)
```

```python
import functools
import math

import jax
import jax.numpy as jnp
from jax import lax
from jax.experimental import pallas as pl
from jax.experimental.pallas import tpu as pltpu

F32 = jnp.float32
BF16 = jnp.bfloat16
I32 = jnp.int32

NORM_EPS = 1e-6
LANES = 128
SUBLANES = 8
MASK_VALUE = -1e30
LOG2E = math.log2(math.e)
ONES_ROWS = 16

ATTN_HEAD_DIM = 64
NUM_BUCKETS = 32
MAX_EXACT = NUM_BUCKETS // 2
MAX_DISTANCE = 128
SSD_HEAD_DIM = 64
SSD_GROUPS = 4
SSD_D_STATE = 128
SSD_CONV = 4
SSD_CHUNK = 128
TOP_K = 2

VMEM_LIMIT = 56 * 1024 * 1024


def _params(*sem):
    return pltpu.CompilerParams(dimension_semantics=sem, vmem_limit_bytes=VMEM_LIMIT)


def _sigmoid(x):
    return 1.0 / (1.0 + jnp.exp(-x))


def _rms_rows(x, gain):
    ms = jnp.mean(x * x, axis=-1, keepdims=True)
    return x * lax.rsqrt(ms + NORM_EPS) * gain


def _attn_inproj_kernel(x_ref, g_ref, w_ref, qkg_ref, bd_ref, qt_ref, k_ref, vt_ref, hn_ref):
    hn_ref[...] = _rms_rows(x_ref[...], g_ref[...]).astype(BF16)
    hn = hn_ref[...]
    width = k_ref.shape[1]
    bd = bd_ref[...]
    w2 = bd.shape[0]

    def head_norm(y, gain_row):
        y2 = (y * y).astype(BF16)
        ms = jnp.concatenate(
            [jnp.dot(y2[:, c * w2:(c + 1) * w2], bd, preferred_element_type=F32)
             for c in range(width // w2)], axis=1)
        return y * lax.rsqrt(ms + NORM_EPS) * gain_row

    q = jnp.dot(hn, w_ref[:, 0:width], preferred_element_type=F32)
    qt_ref[...] = head_norm(q, qkg_ref[0:1, :]).T.astype(BF16)
    k = jnp.dot(hn, w_ref[:, width:2 * width], preferred_element_type=F32)
    k_ref[...] = head_norm(k, qkg_ref[1:2, :]).astype(BF16)
    v = jnp.dot(hn, w_ref[:, 2 * width:3 * width], preferred_element_type=F32)
    vt = v.astype(BF16).T
    hd = 2 * ATTN_HEAD_DIM
    ones = jnp.ones((ONES_ROWS, vt.shape[1]), BF16)
    for h in range(width // hd):
        r0 = h * (hd + ONES_ROWS)
        vt_ref[r0:r0 + hd, :] = vt[h * hd:(h + 1) * hd, :]
        vt_ref[r0 + hd:r0 + hd + ONES_ROWS, :] = ones


def attn_inproj(x, gain, w_in, q_gain, k_gain, tm=512):
    t, d = x.shape
    width = w_in.shape[1] // 3
    tm = min(tm, t)
    reps = width // ATTN_HEAD_DIM
    qk_gain = jnp.stack([jnp.tile(q_gain, reps) * (ATTN_HEAD_DIM ** -0.5 * LOG2E),
                         jnp.tile(k_gain, reps)]).astype(F32)
    hd = 2 * ATTN_HEAD_DIM
    vt_rows = (width // hd) * (hd + ONES_ROWS)
    bw = 256
    grp = jnp.arange(bw) // ATTN_HEAD_DIM
    bd = jnp.where(grp[:, None] == grp[None, :], 1.0 / ATTN_HEAD_DIM, 0.0).astype(BF16)
    return pl.pallas_call(
        _attn_inproj_kernel,
        name="attn_inproj",
        grid=(t // tm,),
        in_specs=[
            pl.BlockSpec((tm, d), lambda i: (i, 0)),
            pl.BlockSpec((1, d), lambda i: (0, 0)),
            pl.BlockSpec((d, 3 * width), lambda i: (0, 0)),
            pl.BlockSpec((2, width), lambda i: (0, 0)),
            pl.BlockSpec((bw, bw), lambda i: (0, 0)),
        ],
        out_specs=[pl.BlockSpec((width, tm), lambda i: (0, i)),
                   pl.BlockSpec((tm, width), lambda i: (i, 0)),
                   pl.BlockSpec((vt_rows, tm), lambda i: (0, i))],
        out_shape=[jax.ShapeDtypeStruct((width, t), BF16),
                   jax.ShapeDtypeStruct((t, width), BF16),
                   jax.ShapeDtypeStruct((vt_rows, t), BF16)],
        scratch_shapes=[pltpu.VMEM((tm, d), BF16)],
        compiler_params=_params("parallel"),
    )(x, gain.reshape(1, d), w_in, qk_gain, bd)


def _bias_tiles_kernel(tab_ref, o_ref, *, tq):
    h = pl.program_id(0)
    r = lax.broadcasted_iota(I32, (tq, tq), 0)
    c = lax.broadcasted_iota(I32, (tq, tq), 1)
    far = tab_ref[NUM_BUCKETS - 1, h]
    for d in range(2):
        dist = c - r + d * tq
        n = jnp.maximum(dist, 0)
        nf = jnp.maximum(n, 1).astype(F32)
        large = MAX_EXACT + (jnp.log(nf / MAX_EXACT) / math.log(MAX_DISTANCE / MAX_EXACT)
                             * (NUM_BUCKETS - MAX_EXACT)).astype(I32)
        large = jnp.minimum(large, NUM_BUCKETS - 1)
        bucket = jnp.where(n < MAX_EXACT, n, large)
        bias = jnp.zeros((tq, tq), F32)
        for b in range(NUM_BUCKETS):
            bias = jnp.where(bucket == b, tab_ref[b, h], bias)
        o_ref[0, d] = jnp.where(dist >= 0, (bias - far) * LOG2E, MASK_VALUE)


def bias_tiles(rel_bias, tq):
    nb, h = rel_bias.shape
    return pl.pallas_call(
        functools.partial(_bias_tiles_kernel, tq=tq),
        name="bias_tiles",
        grid=(h,),
        in_specs=[pl.BlockSpec(memory_space=pltpu.SMEM)],
        out_specs=pl.BlockSpec((1, 2, tq, tq), lambda i: (i, 0, 0, 0)),
        out_shape=jax.ShapeDtypeStruct((h, 2, tq, tq), F32),
        compiler_params=_params("parallel"),
    )(rel_bias.astype(F32))


def _attn_kernel(*refs, tq, cw, lambda_init):
    n_q = refs[1].shape[0] // tq

    def q_body(qi, carry):
        _attn_q_block(qi, *refs, tq=tq, cw=cw, lambda_init=lambda_init)
        return carry

    lax.fori_loop(0, n_q, q_body, 0)


def _attn_q_block(qi, qt_ref, k_ref, vt_ref, bias_ref, lam_ref, sub_ref, o_ref,
                  m_ref, smax_ref, acc_ref, qbd_ref, s_ref, *, tq, cw, lambda_init):
    d = ATTN_HEAD_DIM
    hd = 2 * d
    q0 = pl.multiple_of(qi * tq, tq)
    qt = qt_ref[:, pl.ds(q0, tq)]
    row = lax.broadcasted_iota(I32, qt.shape, 0)
    zero = jnp.zeros_like(qt)
    qbd_ref[:, :tq] = jnp.where(row < d, qt, zero)
    qbd_ref[:, tq:] = jnp.where(row >= d, qt, zero)

    m_ref[...] = jnp.full(m_ref.shape, MASK_VALUE, F32)
    acc_ref[...] = jnp.zeros(acc_ref.shape, F32)

    n_chunks = 2 * tq // cw

    def block_start(j):
        return pl.multiple_of(j * tq, tq)

    def scores(j, c):
        kb = k_ref[pl.ds(block_start(j), tq), :]
        s = jnp.dot(kb, qbd_ref[:, c * cw:(c + 1) * cw], preferred_element_type=F32)
        s_ref[:, c * cw:(c + 1) * cw] = s
        smax_ref[:, c * cw:(c + 1) * cw] = jnp.max(s, axis=0, keepdims=True)

    def stage(j, bias_idx, has_next):
        vtb = vt_ref[:, pl.ds(block_start(j), tq)]
        for c in range(n_chunks):
            cols = slice(c * cw, (c + 1) * cw)
            s = s_ref[:, cols]
            if bias_idx is None:
                s_max = smax_ref[:, cols]
            else:
                b0 = (c * cw) % tq
                s = s + bias_ref[0, bias_idx, :, b0:b0 + cw]
                s_max = jnp.max(s, axis=0, keepdims=True)
            m_prev = m_ref[:, cols]
            m_new = jnp.maximum(m_prev, s_max)
            p = jnp.exp2(s - m_new)
            if has_next:
                scores(j + 1, c)
            alpha = jnp.exp2(m_prev - m_new)
            acc_ref[:, cols] = alpha * acc_ref[:, cols] + jnp.dot(vtb, p.astype(BF16),
                                                                  preferred_element_type=F32)
            m_ref[:, cols] = m_new

    for c in range(n_chunks):
        scores(0, c)

    n_far = jnp.maximum(qi - 1, 0)

    def far_quad(i, carry):
        for u in range(4):
            stage(4 * i + u, None, True)
        return carry

    lax.fori_loop(0, n_far // 4, far_quad, 0)

    @pl.when(n_far % 4 >= 2)
    def _():
        first = n_far // 4 * 4
        stage(first, None, True)
        stage(first + 1, None, True)

    def finish(blocks):
        for idx, (j, bias_idx) in enumerate(blocks):
            stage(j, bias_idx, idx + 1 < len(blocks))
        o = acc_ref[0:hd, :] / acc_ref[hd:hd + 1, :]
        lp = lam_ref[...]
        lam = (jnp.exp(jnp.sum(lp[0:1] * lp[1:2], axis=-1, keepdims=True))
               - jnp.exp(jnp.sum(lp[2:3] * lp[3:4], axis=-1, keepdims=True)) + lambda_init)
        o = o[:, :tq] - lam * o[:, tq:]
        ms = jnp.mean(o * o, axis=0, keepdims=True)
        o = o * lax.rsqrt(ms + NORM_EPS) * sub_ref[...] * (1.0 - lambda_init)
        o_ref[pl.ds(q0, tq), :] = o.T.astype(o_ref.dtype)

    @pl.when(qi == 0)
    def _():
        finish([(qi, 0)])

    @pl.when((qi >= 1) & (n_far % 2 == 0))
    def _():
        finish([(qi - 1, 1), (qi, 0)])

    @pl.when(n_far % 2 == 1)
    def _():
        finish([(n_far - 1, None), (qi - 1, 1), (qi, 0)])


def diff_attention_core(qt, k, vt, bias, lam_params, subln_w, lambda_init, tq, cw=512):
    t, width = k.shape
    hd = 2 * ATTN_HEAD_DIM
    heads = width // hd
    vrows = hd + ONES_ROWS
    assert t % tq == 0 and tq >= MAX_DISTANCE and vt.shape[0] == heads * vrows
    return pl.pallas_call(
        functools.partial(_attn_kernel, tq=tq, cw=min(cw, tq), lambda_init=lambda_init),
        name="diff_attn",
        grid=(heads,),
        in_specs=[
            pl.BlockSpec((hd, t), lambda h: (h, 0)),
            pl.BlockSpec((t, hd), lambda h: (0, h)),
            pl.BlockSpec((vrows, t), lambda h: (h, 0)),
            pl.BlockSpec((1, 2, tq, tq), lambda h: (h, 0, 0, 0)),
            pl.BlockSpec((4, ATTN_HEAD_DIM), lambda h: (0, 0)),
            pl.BlockSpec((hd, 1), lambda h: (0, 0)),
        ],
        out_specs=pl.BlockSpec((t, hd), lambda h: (0, h)),
        out_shape=jax.ShapeDtypeStruct((t, width), BF16),
        scratch_shapes=[pltpu.VMEM((1, 2 * tq), F32), pltpu.VMEM((1, 2 * tq), F32),
                        pltpu.VMEM((vrows, 2 * tq), F32), pltpu.VMEM((hd, 2 * tq), BF16),
                        pltpu.VMEM((tq, 2 * tq), F32)],
        compiler_params=_params("parallel"),
    )(qt, k, vt, bias, lam_params.astype(F32), subln_w.reshape(hd, 1).astype(F32))


def _proj_residual_kernel(x_ref, a_ref, w_ref, o_ref):
    o_ref[...] = x_ref[...] + jnp.dot(a_ref[...], w_ref[...], preferred_element_type=F32)


def proj_residual(x, a, w, tm=1024):
    t, d = x.shape
    k = a.shape[1]
    tm = min(tm, t)
    return pl.pallas_call(
        _proj_residual_kernel,
        name="proj_residual",
        grid=(t // tm,),
        in_specs=[
            pl.BlockSpec((tm, d), lambda i: (i, 0)),
            pl.BlockSpec((tm, k), lambda i: (i, 0)),
            pl.BlockSpec((k, d), lambda i: (0, 0)),
        ],
        out_specs=pl.BlockSpec((tm, d), lambda i: (i, 0)),
        out_shape=jax.ShapeDtypeStruct((t, d), F32),
        compiler_params=_params("parallel"),
    )(x, a, w)


def _swiglu_step(hn, wg, wu, wd):
    gate = jnp.dot(hn, wg, preferred_element_type=F32)
    up = jnp.dot(hn, wu, preferred_element_type=F32)
    act = gate * _sigmoid(gate) * up
    return jnp.dot(act.astype(BF16), wd, preferred_element_type=F32)


def _ffn_kernel(te_ref, first_ref, nu_ref, *refs, layer, normalize, tf):
    if normalize:
        x_ref, g_ref, wg_hbm, wu_hbm, wd_hbm, o_ref = refs[:6]
    else:
        x_ref, wg_hbm, wu_hbm, wd_hbm, o_ref = refs[:5]
    hn_ref, act_ref, wg_res, wu_res, wd_res, wg_stg, wu_stg, wd_stg, sem = refs[-9:]
    i = pl.program_id(0)
    nf = wg_res.shape[0]
    e = te_ref[i]

    def block_copies(jb, slot):
        f0 = jb * tf
        return (pltpu.make_async_copy(wg_hbm.at[layer, e, :, pl.ds(f0, tf)], wg_stg.at[slot], sem.at[0, slot]),
                pltpu.make_async_copy(wu_hbm.at[layer, e, :, pl.ds(f0, tf)], wu_stg.at[slot], sem.at[1, slot]),
                pltpu.make_async_copy(wd_hbm.at[layer, e, pl.ds(f0, tf), :], wd_stg.at[slot], sem.at[2, slot]))

    def tile(stream_weights):
        x = x_ref[...]
        if normalize:
            hn_ref[...] = _rms_rows(x, g_ref[...]).astype(BF16)
        else:
            hn_ref[...] = x.astype(BF16)
        hn = hn_ref[...]
        if stream_weights:
            for cp in block_copies(0, 0):
                cp.start()
        for jb in range(nf):
            if stream_weights:
                slot = jb % 2
                for cp in block_copies(jb, slot):
                    cp.wait()
                if jb + 1 < nf:
                    for cp in block_copies(jb + 1, 1 - slot):
                        cp.start()
                wg_res[jb] = wg_stg[slot].astype(BF16)
                wu_res[jb] = wu_stg[slot].astype(BF16)
                wd_res[jb * tf:(jb + 1) * tf, :] = wd_stg[slot].astype(BF16)
            gate = jnp.dot(hn, wg_res[jb], preferred_element_type=F32)
            up = jnp.dot(hn, wu_res[jb], preferred_element_type=F32)
            act_ref[:, jb * tf:(jb + 1) * tf] = (gate * _sigmoid(gate) * up).astype(BF16)
        y = jnp.dot(act_ref[...], wd_res[...], preferred_element_type=F32)
        o_ref[...] = x + y if normalize else y

    @pl.when(i < nu_ref[0])
    def _():
        @pl.when(first_ref[i] == 1)
        def _():
            tile(True)

        @pl.when(first_ref[i] != 1)
        def _():
            tile(False)

    @pl.when(i >= nu_ref[0])
    def _():
        o_ref[...] = jnp.zeros(o_ref.shape, F32)


def ffn_tiles(x, tile_expert, first, n_used, w_gate, w_up, w_down, layer, gain=None, tm=512, tf=512):
    rows, d = x.shape
    f = w_gate.shape[3]
    nf = f // tf
    assert rows % tm == 0 and f % tf == 0
    normalize = gain is not None

    def row_tile(i, te, fr, nu):
        return (jnp.maximum(jnp.minimum(i, nu[0] - 1), 0), 0)

    in_specs = [pl.BlockSpec((tm, d), row_tile)]
    args = [x]
    if normalize:
        in_specs.append(pl.BlockSpec((1, d), lambda i, te, fr, nu: (0, 0)))
        args.append(gain.reshape(1, d))
    in_specs += [pl.BlockSpec(memory_space=pl.ANY)] * 3
    return pl.pallas_call(
        functools.partial(_ffn_kernel, layer=layer, normalize=normalize, tf=tf),
        name="ffn",
        grid_spec=pltpu.PrefetchScalarGridSpec(
            num_scalar_prefetch=3,
            grid=(rows // tm,),
            in_specs=in_specs,
            out_specs=pl.BlockSpec((tm, d), lambda i, te, fr, nu: (i, 0)),
            scratch_shapes=[pltpu.VMEM((tm, d), BF16), pltpu.VMEM((tm, f), BF16),
                            pltpu.VMEM((nf, d, tf), BF16), pltpu.VMEM((nf, d, tf), BF16),
                            pltpu.VMEM((f, d), BF16),
                            pltpu.VMEM((2, d, tf), F32), pltpu.VMEM((2, d, tf), F32),
                            pltpu.VMEM((2, tf, d), F32),
                            pltpu.SemaphoreType.DMA((3, 2))],
        ),
        out_shape=jax.ShapeDtypeStruct((rows, d), F32),
        compiler_params=_params("arbitrary"),
    )(tile_expert, first, n_used, *args, w_gate, w_up, w_down)


def ffn_residual(x, gain, w_gate, w_up, w_down, layer, tm=512):
    n_tiles = x.shape[0] // min(tm, x.shape[0])
    tiles = jnp.arange(n_tiles, dtype=I32)
    return ffn_tiles(x, jnp.zeros((n_tiles,), I32), (tiles == 0).astype(I32),
                     jnp.full((1,), n_tiles, I32), w_gate[:, None], w_up[:, None], w_down[:, None],
                     layer, gain=gain, tm=min(tm, x.shape[0]))


R_E1, R_E2, R_RANK1, R_RANK2, R_W1, R_W2 = range(6)


def _router_kernel(x_ref, g_ref, r_ref, tri_ref, route_ref, cnt_ref, base_ref, *, n_experts):
    @pl.when(pl.program_id(0) == 0)
    def _():
        base_ref[...] = jnp.zeros(base_ref.shape, F32)

    hn = _rms_rows(x_ref[...], g_ref[...])
    hn_hi = hn.astype(BF16)
    hn_lo = (hn - hn_hi.astype(F32)).astype(BF16)
    logits = (jnp.dot(hn_hi, r_ref[0], preferred_element_type=F32)
              + jnp.dot(hn_hi, r_ref[1], preferred_element_type=F32)
              + jnp.dot(hn_lo, r_ref[0], preferred_element_type=F32))
    lane = lax.broadcasted_iota(I32, logits.shape, 1)
    lg = jnp.where(lane < n_experts, logits, MASK_VALUE)
    m1 = jnp.max(lg, axis=-1, keepdims=True)
    i1 = jnp.min(jnp.where(lg == m1, lane, LANES), axis=-1, keepdims=True)
    lg2 = jnp.where(lane == i1, MASK_VALUE, lg)
    m2 = jnp.max(lg2, axis=-1, keepdims=True)
    i2 = jnp.min(jnp.where(lg2 == m2, lane, LANES), axis=-1, keepdims=True)
    e2 = jnp.exp(m2 - m1)
    w1 = 1.0 / (1.0 + e2)
    w2 = e2 / (1.0 + e2)

    sel1 = lane == i1
    sel2 = lane == i2
    mask = jnp.where(sel1 | sel2, 1.0, 0.0)
    rank = base_ref[...] + jnp.dot(tri_ref[...], mask.astype(BF16), preferred_element_type=F32)
    rank1 = jnp.sum(jnp.where(sel1, rank, 0.0), axis=-1, keepdims=True)
    rank2 = jnp.sum(jnp.where(sel2, rank, 0.0), axis=-1, keepdims=True)
    base_ref[...] += jnp.sum(mask, axis=0, keepdims=True)

    rec = jnp.zeros(logits.shape, F32)
    for slot, val in ((R_E1, i1.astype(F32)), (R_E2, i2.astype(F32)), (R_RANK1, rank1),
                      (R_RANK2, rank2), (R_W1, w1), (R_W2, w2)):
        rec = jnp.where(lane == slot, val, rec)
    route_ref[...] = rec
    cnt_ref[...] = jnp.broadcast_to(base_ref[...], cnt_ref.shape)


def router(x, gain, w_router, tm=1024):
    t, d = x.shape
    ne = w_router.shape[1]
    tm = min(tm, t)
    r_pad = jnp.zeros((d, LANES), F32).at[:, :ne].set(w_router.astype(F32))
    r_hi = r_pad.astype(BF16)
    r_split = jnp.stack([r_hi, (r_pad - r_hi.astype(F32)).astype(BF16)])
    idx = jnp.arange(tm)
    tri = (idx[None, :] < idx[:, None]).astype(BF16)
    return pl.pallas_call(
        functools.partial(_router_kernel, n_experts=ne),
        name="router",
        grid=(t // tm,),
        in_specs=[pl.BlockSpec((tm, d), lambda i: (i, 0)),
                  pl.BlockSpec((1, d), lambda i: (0, 0)),
                  pl.BlockSpec((2, d, LANES), lambda i: (0, 0, 0)),
                  pl.BlockSpec((tm, tm), lambda i: (0, 0))],
        out_specs=[pl.BlockSpec((tm, LANES), lambda i: (i, 0)),
                   pl.BlockSpec((8, LANES), lambda i: (0, 0))],
        out_shape=[jax.ShapeDtypeStruct((t, LANES), F32),
                   jax.ShapeDtypeStruct((8, LANES), F32)],
        scratch_shapes=[pltpu.VMEM((1, LANES), F32)],
        compiler_params=_params("arbitrary"),
    )(x, gain.reshape(1, d), r_split, tri)


def _row_copy(src, r, dst, p, sem):
    return pltpu.make_async_copy(src.at[pl.ds(r, 1), :], dst.at[pl.ds(p, 1), :], sem)


def _dispatch_kernel(pos_ref, zt_ref, x_ref, g_ref, xs_ref, hn_ref, zero_ref, sem, zsem, *, tm):
    tmg = zero_ref.shape[0]

    def zero_tile(z):
        r0 = pl.multiple_of(zt_ref[z], tmg)
        return pltpu.make_async_copy(zero_ref, xs_ref.at[pl.ds(r0, tmg), :], zsem)

    @pl.when(pl.program_id(0) == 0)
    def _():
        zero_ref[...] = jnp.zeros(zero_ref.shape, F32)
        for z in range(zt_ref.shape[0]):
            @pl.when(zt_ref[z] >= 0)
            def _():
                zero_tile(z).start()
        for z in range(zt_ref.shape[0]):
            @pl.when(zt_ref[z] >= 0)
            def _():
                zero_tile(z).wait()

    base = pl.program_id(0) * tm
    hn_ref[...] = _rms_rows(x_ref[...], g_ref[...])

    def issue(r, carry):
        for k in range(TOP_K):
            _row_copy(hn_ref, r, xs_ref, pos_ref[TOP_K * (base + r) + k], sem).start(priority=k % 2)
        return carry

    lax.fori_loop(0, tm, issue, 0, unroll=8)

    def drain(r, carry):
        for k in range(TOP_K):
            _row_copy(hn_ref, r, xs_ref, pos_ref[TOP_K * (base + r) + k], sem).wait()
        return carry

    lax.fori_loop(0, tm, drain, 0, unroll=8)


def dispatch(x, gain, pos, zero_tiles, n_rows, tmg, tm=512):
    t, d = x.shape
    tm = min(tm, t)
    return pl.pallas_call(
        functools.partial(_dispatch_kernel, tm=tm),
        name="dispatch",
        grid_spec=pltpu.PrefetchScalarGridSpec(
            num_scalar_prefetch=2,
            grid=(t // tm,),
            in_specs=[pl.BlockSpec((tm, d), lambda i, pos, zt: (i, 0)),
                      pl.BlockSpec((1, d), lambda i, pos, zt: (0, 0))],
            out_specs=pl.BlockSpec(memory_space=pl.ANY),
            scratch_shapes=[pltpu.VMEM((tm, d), F32), pltpu.VMEM((tmg, d), F32),
                            pltpu.SemaphoreType.DMA(()), pltpu.SemaphoreType.DMA(())],
        ),
        out_shape=jax.ShapeDtypeStruct((n_rows, d), F32),
        compiler_params=_params("arbitrary"),
    )(pos, zero_tiles, x, gain.reshape(1, d))


def _combine_kernel(pos_ref, x_ref, route_ref, ys_ref, o_ref, buf_ref, sem, *, tm):
    base = pl.program_id(0) * tm

    def issue(r, carry):
        for k in range(TOP_K):
            _row_copy(ys_ref, pos_ref[TOP_K * (base + r) + k], buf_ref.at[k], r, sem).start(priority=k % 2)
        return carry

    lax.fori_loop(0, tm, issue, 0, unroll=8)

    def drain(r, carry):
        for k in range(TOP_K):
            _row_copy(ys_ref, pos_ref[TOP_K * (base + r) + k], buf_ref.at[k], r, sem).wait()
        return carry

    lax.fori_loop(0, tm, drain, 0, unroll=8)

    rec = route_ref[...]
    lane = lax.broadcasted_iota(I32, rec.shape, 1)
    w1 = jnp.sum(jnp.where(lane == R_W1, rec, 0.0), axis=-1, keepdims=True)
    w2 = jnp.sum(jnp.where(lane == R_W2, rec, 0.0), axis=-1, keepdims=True)
    o_ref[...] = x_ref[...] + w1 * buf_ref[0] + w2 * buf_ref[1]


def combine(x, route, ys, pos, tm=512):
    t, d = x.shape
    tm = min(tm, t)
    return pl.pallas_call(
        functools.partial(_combine_kernel, tm=tm),
        name="combine",
        grid_spec=pltpu.PrefetchScalarGridSpec(
            num_scalar_prefetch=1,
            grid=(t // tm,),
            in_specs=[pl.BlockSpec((tm, d), lambda i, pos: (i, 0)),
                      pl.BlockSpec((tm, LANES), lambda i, pos: (i, 0)),
                      pl.BlockSpec(memory_space=pl.ANY)],
            out_specs=pl.BlockSpec((tm, d), lambda i, pos: (i, 0)),
            scratch_shapes=[pltpu.VMEM((TOP_K, tm, d), F32), pltpu.SemaphoreType.DMA(())],
        ),
        out_shape=jax.ShapeDtypeStruct((t, d), F32),
        compiler_params=_params("arbitrary"),
    )(pos, x, route, ys)


def _ssd_kernel(*refs, d_inner):
    prev_ref, state_ref = refs[-3], refs[-2]

    @pl.when(pl.program_id(0) == 0)
    def _():
        prev_ref[...] = jnp.zeros(prev_ref.shape, F32)
        state_ref[...] = jnp.zeros(state_ref.shape, F32)

    def chunk_body(c, carry):
        _ssd_chunk(c, *refs, d_inner=d_inner)
        return carry

    lax.fori_loop(0, refs[0].shape[0] // SSD_CHUNK, chunk_body, 0)


def _ssd_chunk(c, xbc_ref, z_ref, dt_ref, cw_ref, cb_ref, dtb_ref, alog_ref, dsk_ref, nw_ref,
               o_ref, prev_ref, state_ref, y_ref, *, d_inner):
    L = SSD_CHUNK
    gn = SSD_GROUPS * SSD_D_STATE
    n_pairs = d_inner // LANES
    pairs_per_group = n_pairs // SSD_GROUPS
    rows = pl.ds(pl.multiple_of(c * L, L), L)

    x = xbc_ref[rows, :]
    xp = prev_ref[...]
    row1 = lax.broadcasted_iota(I32, (SUBLANES, 1), 0)
    cw = cw_ref[...]
    acc = cb_ref[...] + cw[SSD_CONV - 1:SSD_CONV] * x
    for k in range(1, SSD_CONV):
        shifted = pltpu.roll(x, k, 0)
        head = jnp.where(row1 < k, pltpu.roll(xp, k, 0), shifted[:SUBLANES])
        shifted = jnp.concatenate([head, shifted[SUBLANES:]], axis=0)
        acc = acc + cw[SSD_CONV - 1 - k:SSD_CONV - k] * shifted
    prev_ref[...] = x[L - SUBLANES:]
    xc = acc * _sigmoid(acc)

    dtr = dt_ref[rows, :] + dtb_ref[...]
    dtv = jnp.maximum(dtr, 0.0) + jnp.log1p(jnp.exp(-jnp.abs(dtr)))
    da = dtv * (-jnp.exp(alog_ref[...]))
    rowh = lax.broadcasted_iota(I32, da.shape, 0)
    acs = da
    s = 1
    while s < L:
        acs = acs + jnp.where(rowh >= s, pltpu.roll(acs, s, 0), 0.0)
        s *= 2
    acs_t = acs.T
    tot = acs[L - 1:L, :]
    decay_to_end = jnp.exp(tot - acs)
    decay_from_start = jnp.exp(acs)
    tot_col = jnp.exp(acs_t[:, L - 1:L])

    lane = lax.broadcasted_iota(I32, (L, LANES), 1)
    rowl = lax.broadcasted_iota(I32, (L, LANES), 0)
    left = lane < SSD_HEAD_DIM
    causal = rowl >= lane

    def per_head(a, h0):
        return jnp.where(left, a[:, h0:h0 + 1], a[:, h0 + 1:h0 + 2])

    for g in range(SSD_GROUPS):
        bg = xc[:, d_inner + g * SSD_D_STATE:d_inner + (g + 1) * SSD_D_STATE].astype(BF16)
        cg = xc[:, d_inner + gn + g * SSD_D_STATE:d_inner + gn + (g + 1) * SSD_D_STATE].astype(BF16)
        cbm = lax.dot_general(cg, bg, (((1,), (1,)), ((), ())), preferred_element_type=F32)
        for jp in range(pairs_per_group):
            pj = g * pairs_per_group + jp
            h0 = 2 * pj
            xs_pair = xc[:, pj * LANES:(pj + 1) * LANES]
            xd = xs_pair * per_head(dtv, h0)
            y = jnp.zeros((L, LANES), F32)
            for half, hh in enumerate((h0, h0 + 1)):
                diff = acs[:, hh:hh + 1] - acs_t[hh:hh + 1, :]
                dec = jnp.exp(jnp.where(causal, diff, MASK_VALUE))
                in_half = left if half == 0 else jnp.logical_not(left)
                y = y + jnp.dot((cbm * dec).astype(BF16),
                                jnp.where(in_half, xd, 0.0).astype(BF16),
                                preferred_element_type=F32)
            s_old = state_ref[pj]
            y_off = lax.dot_general(cg, s_old.astype(BF16), (((1,), (1,)), ((), ())),
                                    preferred_element_type=F32)
            y = y + y_off * per_head(decay_from_start, h0)
            s_new = lax.dot_general((xd * per_head(decay_to_end, h0)).astype(BF16), bg,
                                    (((0,), (0,)), ((), ())), preferred_element_type=F32)
            carry = jnp.where(rowl < SSD_HEAD_DIM, tot_col[h0:h0 + 1, :], tot_col[h0 + 1:h0 + 2, :])
            state_ref[pj] = carry * s_old + s_new
            y_ref[:, pj * LANES:(pj + 1) * LANES] = y + dsk_ref[:, pj * LANES:(pj + 1) * LANES] * xs_pair

    z = z_ref[rows, :].astype(F32)
    yg = y_ref[...] * (z * _sigmoid(z))
    gw = d_inner // SSD_GROUPS
    for g in range(SSD_GROUPS):
        blk = yg[:, g * gw:(g + 1) * gw]
        o_ref[rows, g * gw:(g + 1) * gw] = _rms_rows(blk, nw_ref[:, g * gw:(g + 1) * gw]).astype(o_ref.dtype)


def ssd_core(xbc, z, dt, conv_w, conv_b, dt_bias, a_log, d_skip, norm_w, chunks_per_step=4):
    t = xbc.shape[0]
    d_inner = z.shape[1]
    conv_dim = xbc.shape[1]
    heads = d_inner // SSD_HEAD_DIM
    assert t % SSD_CHUNK == 0 and heads <= LANES
    while (t // SSD_CHUNK) % chunks_per_step:
        chunks_per_step //= 2
    L = SSD_CHUNK * chunks_per_step

    def lanes(v):
        return jnp.zeros((1, LANES), F32).at[0, :heads].set(v.astype(F32))

    row = lambda i: (i, 0)
    fixed = lambda i: (0, 0)
    return pl.pallas_call(
        functools.partial(_ssd_kernel, d_inner=d_inner),
        name="ssd",
        grid=(t // L,),
        in_specs=[
            pl.BlockSpec((L, conv_dim), row),
            pl.BlockSpec((L, d_inner), row),
            pl.BlockSpec((L, LANES), row),
            pl.BlockSpec((SSD_CONV, conv_dim), fixed),
            pl.BlockSpec((1, conv_dim), fixed),
            pl.BlockSpec((1, LANES), fixed),
            pl.BlockSpec((1, LANES), fixed),
            pl.BlockSpec((1, d_inner), fixed),
            pl.BlockSpec((1, d_inner), fixed),
        ],
        out_specs=pl.BlockSpec((L, d_inner), row),
        out_shape=jax.ShapeDtypeStruct((t, d_inner), BF16),
        scratch_shapes=[pltpu.VMEM((SUBLANES, conv_dim), F32),
                        pltpu.VMEM((d_inner // LANES, LANES, SSD_D_STATE), F32),
                        pltpu.VMEM((SSD_CHUNK, d_inner), F32)],
        compiler_params=_params("arbitrary"),
    )(xbc, z, dt, conv_w.astype(F32), conv_b.reshape(1, conv_dim).astype(F32),
      lanes(dt_bias), lanes(a_log),
      jnp.repeat(d_skip.astype(F32), SSD_HEAD_DIM).reshape(1, d_inner),
      norm_w.reshape(1, d_inner).astype(F32))


def lambda_init_for(layer_idx):
    return 0.8 - 0.6 * math.exp(-0.3 * layer_idx)


def _ssd_inproj_kernel(x_ref, g_ref, w_hbm, wdt_ref, z_ref, xbc_ref, dt_ref,
                       hn_ref, wres_ref, stg_ref, sem, *, layer, tn, d_inner):
    nb = wres_ref.shape[0]

    def block_copy(jb, slot):
        c0 = pl.multiple_of(jb * tn, tn)
        return pltpu.make_async_copy(w_hbm.at[layer, :, pl.ds(c0, tn)], stg_ref.at[slot], sem.at[slot])

    @pl.when(pl.program_id(0) == 0)
    def _():
        block_copy(0, 0).start()

        def body(jb, carry):
            slot = jb % 2
            block_copy(jb, slot).wait()

            @pl.when(jb + 1 < nb)
            def _():
                block_copy(jb + 1, 1 - slot).start()

            wres_ref[jb] = stg_ref[slot].astype(BF16)
            return carry

        lax.fori_loop(0, nb, body, 0)

    hn_ref[...] = _rms_rows(x_ref[...], g_ref[...]).astype(BF16)
    hn = hn_ref[...]
    for jb in range(nb):
        y = jnp.dot(hn, wres_ref[jb], preferred_element_type=F32)
        c0 = jb * tn
        if c0 < d_inner:
            z_ref[:, c0:c0 + tn] = y.astype(z_ref.dtype)
        else:
            xbc_ref[:, c0 - d_inner:c0 - d_inner + tn] = y
    dt_ref[...] = jnp.dot(hn, wdt_ref[...], preferred_element_type=F32)


def ssd_inproj(x, gain, w_in_all, layer, d_inner, conv_dim, heads, tm=512, tn=512):
    t, d = x.shape
    tm = min(tm, t)
    n_main = d_inner + conv_dim
    assert t % tm == 0 and d_inner % tn == 0 and conv_dim % tn == 0 and heads <= LANES
    w_dt = jnp.zeros((d, LANES), BF16).at[:, :heads].set(w_in_all[layer, :, n_main:].astype(BF16))
    row = lambda i: (i, 0)
    fixed = lambda i: (0, 0)
    return pl.pallas_call(
        functools.partial(_ssd_inproj_kernel, layer=layer, tn=tn, d_inner=d_inner),
        name="ssd_inproj",
        grid=(t // tm,),
        in_specs=[pl.BlockSpec((tm, d), row),
                  pl.BlockSpec((1, d), fixed),
                  pl.BlockSpec(memory_space=pl.ANY),
                  pl.BlockSpec((d, LANES), fixed)],
        out_specs=[pl.BlockSpec((tm, d_inner), row),
                   pl.BlockSpec((tm, conv_dim), row),
                   pl.BlockSpec((tm, LANES), row)],
        out_shape=[jax.ShapeDtypeStruct((t, d_inner), BF16),
                   jax.ShapeDtypeStruct((t, conv_dim), F32),
                   jax.ShapeDtypeStruct((t, LANES), F32)],
        scratch_shapes=[pltpu.VMEM((tm, d), BF16),
                        pltpu.VMEM((n_main // tn, d, tn), BF16),
                        pltpu.VMEM((2, d, tn), F32),
                        pltpu.SemaphoreType.DMA((2,))],
        compiler_params=_params("arbitrary"),
    )(x, gain.reshape(1, d), w_in_all, w_dt)


def ssd_layer(xt, gain, w_in_all, layer, conv_w, conv_b, dt_bias, a_log, d_skip, norm_w, w_out):
    d_inner = w_out.shape[0]
    conv_dim = conv_w.shape[1]
    heads = dt_bias.shape[0]
    z, xbc, dt = ssd_inproj(xt, gain, w_in_all, layer, d_inner, conv_dim, heads)
    y = ssd_core(xbc, z, dt, conv_w, conv_b, dt_bias, a_log, d_skip, norm_w)
    return proj_residual(xt, y, w_out.astype(BF16))


def moe_layer(xt, gain, w_router, w_gate, w_up, w_down, layer, tmg=512):
    t, d = xt.shape
    ne = w_router.shape[1]
    tmg = min(tmg, t)
    route, counts = router(xt, gain, w_router)
    counts = counts[0, :ne].astype(I32)
    padded = (counts + tmg - 1) // tmg * tmg
    ends = jnp.cumsum(padded)
    starts = ends - padded
    experts = route[:, R_E1:R_E2 + 1].astype(I32)
    ranks = route[:, R_RANK1:R_RANK2 + 1].astype(I32)
    pos = (jnp.take(starts, experts) + ranks).reshape(-1)
    n_rows = TOP_K * t + ne * tmg
    n_tiles = n_rows // tmg
    n_used = (ends[-1] // tmg).astype(I32)
    tile_row = jnp.minimum(jnp.arange(n_tiles, dtype=I32), n_used - 1) * tmg
    tile_expert = jnp.sum(tile_row[:, None] >= ends[None, :], axis=1).astype(I32)
    first = jnp.concatenate([jnp.ones((1,), I32), (tile_expert[1:] != tile_expert[:-1]).astype(I32)])
    tail = (n_used + jnp.arange(ne, dtype=I32)) * tmg
    zero_tiles = jnp.concatenate([jnp.where(padded > 0, ends - tmg, -1),
                                  jnp.where(tail < n_rows, tail, -1)]).astype(I32)
    xs = dispatch(xt, gain, pos, zero_tiles, n_rows, tmg)
    ys = ffn_tiles(xs, tile_expert, first, n_used.reshape(1), w_gate, w_up, w_down, layer, tm=tmg)
    return combine(xt, route, ys, pos)


def kernel(x, rel_bias, norm_mix, norm_ffn, attn_w_in, attn_q_gain, attn_k_gain, attn_lambda, attn_subln, attn_w_out, ffn_w_gate, ffn_w_up, ffn_w_down, ssd_w_in, ssd_conv_w, ssd_conv_b, ssd_dt_bias, ssd_a_log, ssd_d, ssd_norm, ssd_w_out, moe_router, moe_w_gate, moe_w_up, moe_w_down):
    b, s, d = x.shape
    t = b * s
    assert b == 1
    depth = norm_mix.shape[0]
    xt = x.reshape(t, d).astype(F32)
    tq = min(512, t)
    bias = bias_tiles(rel_bias, tq)
    for i in range(depth):
        j = i // 2
        if i % 2 == 0:
            qt, k, vt = attn_inproj(xt, norm_mix[i], attn_w_in[j].astype(BF16), attn_q_gain[j], attn_k_gain[j])
            o = diff_attention_core(qt, k, vt, bias, attn_lambda[j], attn_subln[j], lambda_init_for(i), tq)
            xt = proj_residual(xt, o, attn_w_out[j].astype(BF16))
            xt = ffn_residual(xt, norm_ffn[i], ffn_w_gate, ffn_w_up, ffn_w_down, j)
        else:
            xt = ssd_layer(xt, norm_mix[i], ssd_w_in, j, ssd_conv_w[j], ssd_conv_b[j], ssd_dt_bias[j],
                           ssd_a_log[j], ssd_d[j], ssd_norm[j], ssd_w_out[j])
            xt = moe_layer(xt, norm_ffn[i], moe_router[j], moe_w_gate, moe_w_up, moe_w_down, j)
    return xt.reshape(b, s, d)
```
